```python
import jax, jax.numpy as jnp
from jax import lax
import numpy as np

D_MODEL = 2048
BATCH = 2
SEQ = 4096
DEPTH = 2

HEAD_DIM = 64
MIX_WIDTH = D_MODEL
N_HEADS_TOTAL = MIX_WIDTH // HEAD_DIM
SB_HEADS = N_HEADS_TOTAL // 2
SWA_HEADS = N_HEADS_TOTAL - SB_HEADS
SWA_KV_HEADS = 4
SWA_GROUP = SWA_HEADS // SWA_KV_HEADS
WINDOW = 128
BLOCK = 128
D_FF = -(-8 * D_MODEL // (3 * 256)) * 256
RMS_EPS = 1e-5

SB_W = SB_HEADS * HEAD_DIM
SWA_Q_W = SWA_HEADS * HEAD_DIM
SWA_KV_W = SWA_KV_HEADS * HEAD_DIM
IN_COLS = 3 * SB_W + SWA_Q_W + 2 * SWA_KV_W
SPLITS = (SB_W, 2 * SB_W, 3 * SB_W, 3 * SB_W + SWA_Q_W, 3 * SB_W + SWA_Q_W + SWA_KV_W)

kernel_name = "hymba_stickbreak_swa_sink_alibi_swiglu"


def rmsnorm(x, g):
    xf = x.astype(jnp.float32)
    y = xf * lax.rsqrt(jnp.mean(xf * xf, axis=-1, keepdims=True) + RMS_EPS)
    return (y * g.astype(jnp.float32)).astype(x.dtype)


def stick_breaking_attention(q, k, v):
    B, S, H, Dh = q.shape
    qf = q.astype(jnp.float32) * (Dh ** -0.5)
    kf = k.astype(jnp.float32)
    vf = v.astype(jnp.float32)
    outs = []
    for i in range(S // BLOCK):
        q0 = i * BLOCK
        kend = q0 + BLOCK
        z = jnp.einsum('bqhd,bkhd->bhqk', qf[:, q0:kend], kf[:, :kend])
        t = q0 + jnp.arange(BLOCK)[:, None]
        s = jnp.arange(kend)[None, :]
        causal = s < t
        log_beta = jax.nn.log_sigmoid(z)
        log_one_minus = jnp.where(causal, jax.nn.log_sigmoid(-z), 0.0)
        survive = lax.cumsum(log_one_minus, axis=3, reverse=True) - log_one_minus
        weights = jnp.where(causal, jnp.exp(log_beta + survive), 0.0)
        outs.append(jnp.einsum('bhqk,bkhd->bqhd', weights, vf[:, :kend]))
    return jnp.concatenate(outs, axis=1).astype(q.dtype)


def sliding_window_sink_attention(q, k, v, sinks, slopes):
    B, S, H, Dh = q.shape
    G = k.shape[2]
    R = H // G
    nb = S // BLOCK
    f32 = jnp.float32
    qb = q.astype(f32).reshape(B, nb, BLOCK, G, R, Dh) * (Dh ** -0.5)
    pad = jnp.zeros((B, BLOCK, G, Dh), f32)
    kp = jnp.concatenate([pad, k.astype(f32)], axis=1).reshape(B, nb + 1, BLOCK, G, Dh)
    vp = jnp.concatenate([pad, v.astype(f32)], axis=1).reshape(B, nb + 1, BLOCK, G, Dh)
    kband = jnp.concatenate([kp[:, :-1], kp[:, 1:]], axis=2)
    vband = jnp.concatenate([vp[:, :-1], vp[:, 1:]], axis=2)
    scores = jnp.einsum('bnqgrd,bnkgd->bngrqk', qb, kband)
    qpos = jnp.arange(BLOCK)[:, None] + BLOCK
    kpos = jnp.arange(2 * BLOCK)[None, :]
    dist = (qpos - kpos).astype(f32)
    in_window = (dist >= 0) & (dist < WINDOW)
    blk = jnp.arange(nb)[:, None, None]
    real_key = (blk * BLOCK + kpos[None] - BLOCK) >= 0
    mask = in_window[None] & real_key
    m_h = slopes.astype(f32).reshape(G, R)[:, :, None, None]
    scores = scores - m_h * dist
    scores = jnp.where(mask[None, :, None, None], scores, -jnp.inf)
    sink = sinks.astype(f32).reshape(1, 1, G, R, 1, 1)
    mx = jnp.maximum(jnp.max(scores, axis=-1, keepdims=True), sink)
    p = jnp.exp(scores - mx)
    denom = jnp.sum(p, axis=-1, keepdims=True) + jnp.exp(sink - mx)
    probs = p / denom
    out = jnp.einsum('bngrqk,bnkgd->bnqgrd', probs, vband)
    return out.reshape(B, S, H, Dh).astype(q.dtype)


def setup_inputs(seed: int = 0) -> dict:
    key = jax.random.key(seed)
    ks = jax.random.split(key, 12)
    f32 = jnp.float32
    x = jax.random.normal(ks[0], (BATCH, SEQ, D_MODEL), f32)
    ln_mix = 1.0 + 0.02 * jax.random.normal(ks[1], (DEPTH, D_MODEL), f32)
    w_in = jax.random.normal(ks[2], (DEPTH, D_MODEL, IN_COLS), f32) * D_MODEL ** -0.5
    sb_out_norm = 1.0 + 0.02 * jax.random.normal(ks[3], (DEPTH, SB_W), f32)
    swa_out_norm = 1.0 + 0.02 * jax.random.normal(ks[4], (DEPTH, SWA_Q_W), f32)
    swa_sinks = 0.5 * jax.random.normal(ks[5], (DEPTH, SWA_HEADS), f32)
    w_out = jax.random.normal(ks[6], (DEPTH, MIX_WIDTH, D_MODEL), f32) * (MIX_WIDTH * 2 * DEPTH) ** -0.5
    ln_ffn = 1.0 + 0.02 * jax.random.normal(ks[7], (DEPTH, D_MODEL), f32)
    w_gate_up = jax.random.normal(ks[8], (DEPTH, D_MODEL, 2 * D_FF), f32) * D_MODEL ** -0.5
    w_down = jax.random.normal(ks[9], (DEPTH, D_FF, D_MODEL), f32) * (D_FF * 2 * DEPTH) ** -0.5
    ln_final = 1.0 + 0.02 * jax.random.normal(ks[10], (D_MODEL,), f32)
    return {"x": x, "ln_mix": ln_mix, "w_in": w_in, "sb_out_norm": sb_out_norm,
            "swa_out_norm": swa_out_norm, "swa_sinks": swa_sinks, "w_out": w_out,
            "ln_ffn": ln_ffn, "w_gate_up": w_gate_up, "w_down": w_down, "ln_final": ln_final}


def reference(x, ln_mix, w_in, sb_out_norm, swa_out_norm, swa_sinks, w_out,
              ln_ffn, w_gate_up, w_down, ln_final):
    B, S, _ = x.shape
    slopes = 2.0 ** (-8.0 * (jnp.arange(SWA_HEADS, dtype=jnp.float32) + 1.0) / SWA_HEADS)
    for l in range(DEPTH):
        h = rmsnorm(x, ln_mix[l])
        proj = h @ w_in[l]
        sb_q, sb_k, sb_v, sw_q, sw_k, sw_v = jnp.split(proj, SPLITS, axis=-1)
        sb_o = stick_breaking_attention(
            sb_q.reshape(B, S, SB_HEADS, HEAD_DIM),
            sb_k.reshape(B, S, SB_HEADS, HEAD_DIM),
            sb_v.reshape(B, S, SB_HEADS, HEAD_DIM)).reshape(B, S, SB_W)
        sw_o = sliding_window_sink_attention(
            sw_q.reshape(B, S, SWA_HEADS, HEAD_DIM),
            sw_k.reshape(B, S, SWA_KV_HEADS, HEAD_DIM),
            sw_v.reshape(B, S, SWA_KV_HEADS, HEAD_DIM),
            swa_sinks[l], slopes).reshape(B, S, SWA_Q_W)
        mixed = jnp.concatenate([rmsnorm(sb_o, sb_out_norm[l]),
                                 rmsnorm(sw_o, swa_out_norm[l])], axis=-1)
        x = x + mixed @ w_out[l]
        h = rmsnorm(x, ln_ffn[l])
        gate, up = jnp.split(h @ w_gate_up[l], 2, axis=-1)
        x = x + (jax.nn.silu(gate) * up) @ w_down[l]
    return rmsnorm(x, ln_final)
```

```python
import functools
import math

import jax
import jax.numpy as jnp
from jax import lax
from jax.experimental import pallas as pl
from jax.experimental.pallas import tpu as pltpu

D_MODEL = 2048
HEAD_DIM = 64
SB_HEADS = 16
SWA_HEADS = 16
SWA_KV_HEADS = 4
SWA_GROUP = SWA_HEADS // SWA_KV_HEADS
WINDOW = 128
D_FF = 5632
RMS_EPS = 1e-5

SB_W = SB_HEADS * HEAD_DIM
SWA_Q_W = SWA_HEADS * HEAD_DIM
SWA_KV_W = SWA_KV_HEADS * HEAD_DIM
IN_COLS = 3 * SB_W + SWA_Q_W + 2 * SWA_KV_W
Q_SCALE = HEAD_DIM ** -0.5

LANES = 128
HEAD_PAIR = LANES // HEAD_DIM
VMEM_LIMIT = 56 * 1024 * 1024

F32 = jnp.float32
BF16 = jnp.bfloat16

PROJ_TM, PROJ_TN = 1024, 768
SB_T = 256
SB_SUB = 128
SWA_T = WINDOW
OUT_TM, OUT_TN = 512, 1024
FFN_TM, FFN_TF = 512, 512


def _params(n_axes):
    return pltpu.CompilerParams(
        dimension_semantics=("arbitrary",) * n_axes,
        vmem_limit_bytes=VMEM_LIMIT)


def _rms(x, g):
    ms = jnp.mean(x * x, axis=-1, keepdims=True)
    return x * lax.rsqrt(ms + RMS_EPS) * g


def _dot(a, b):
    return jnp.dot(a, b, preferred_element_type=F32)


def _dot_nt(a, b):
    return lax.dot_general(a, b, (((1,), (1,)), ((), ())),
                           preferred_element_type=F32)


def _norm_proj_kernel(x_ref, g_ref, w_ref, o_ref, h_ref):
    @pl.when(pl.program_id(1) == 0)
    def _():
        h_ref[...] = _rms(x_ref[...], g_ref[...]).astype(BF16)

    o_ref[...] = _dot(h_ref[...], w_ref[...]).astype(o_ref.dtype)


def _norm_proj(x, g, w):
    t, d = x.shape
    n = w.shape[1]
    return pl.pallas_call(
        _norm_proj_kernel,
        grid=(t // PROJ_TM, n // PROJ_TN),
        in_specs=[
            pl.BlockSpec((PROJ_TM, d), lambda i, j: (i, 0)),
            pl.BlockSpec((1, d), lambda i, j: (0, 0)),
            pl.BlockSpec((d, PROJ_TN), lambda i, j: (0, j)),
        ],
        out_specs=pl.BlockSpec((PROJ_TM, PROJ_TN), lambda i, j: (i, j)),
        out_shape=jax.ShapeDtypeStruct((t, n), BF16),
        scratch_shapes=[pltpu.VMEM((PROJ_TM, d), BF16)],
        compiler_params=_params(2),
        name="norm_in_proj",
    )(x, g.reshape(1, d), w)


def _sb_kernel(q_ref, k_ref, v_ref, o_ref, vt_ref):
    i = pl.program_id(2)
    n_kblk = vt_ref.shape[0]

    @pl.when(i == 0)
    def _():
        def tr(c, _):
            r0 = pl.multiple_of(c * SB_T, SB_T)
            vt_ref[c] = v_ref[0, pl.ds(r0, SB_T), :].astype(F32).T.astype(BF16)
            return 0
        lax.fori_loop(0, n_kblk, tr, 0)

    q = q_ref[0] * Q_SCALE
    lane = lax.broadcasted_iota(jnp.int32, q.shape, 1)
    zero = jnp.zeros_like(q)
    qm = [jnp.where((lane // HEAD_DIM) == h, q, zero) for h in range(HEAD_PAIR)]

    r = lax.broadcasted_iota(jnp.int32, (SB_SUB, 2 * SB_SUB), 0)
    j2 = lax.broadcasted_iota(jnp.int32, (SB_SUB, 2 * SB_SUB), 1)
    tri = jnp.where((j2 % SB_SUB) > r, 1.0, 0.0).astype(BF16)

    srow = lax.broadcasted_iota(jnp.int32, (SB_T, SB_T), 0)
    tcol = lax.broadcasted_iota(jnp.int32, (SB_T, SB_T), 1)
    causal = srow < tcol

    n_sub = SB_T // SB_SUB

    def block(kb, vtb, c, acc, qh, diag):
        zt = _dot_nt(kb, qh)
        l = jnp.log(1.0 + jnp.exp(-jnp.abs(zt)))
        ls = jnp.minimum(zt, 0.0) - l
        lom = ls - zt
        if diag:
            lom = jnp.where(causal, lom, 0.0)
        hi = lom.astype(BF16)
        lo = (lom - hi.astype(F32)).astype(BF16)
        survs = [None] * n_sub
        for sb in reversed(range(n_sub)):
            r0, r1 = sb * SB_SUB, (sb + 1) * SB_SUB
            cs = _dot(tri, jnp.concatenate([hi[r0:r1], lo[r0:r1]], axis=0))
            survs[sb] = cs + c
            c = c + (cs[0:1, :] + lom[r0:r0 + 1, :])
        w = jnp.exp(ls + jnp.concatenate(survs, axis=0))
        if diag:
            w = jnp.where(causal, w, 0.0)
        acc = acc + _dot(vtb, w.astype(BF16))
        return c, acc

    q0 = pl.multiple_of(i * SB_T, SB_T)
    kb = k_ref[0, pl.ds(q0, SB_T), :]
    vtb = vt_ref[i]
    state = []
    for h in range(HEAD_PAIR):
        c0 = jnp.zeros((1, SB_T), F32)
        a0 = jnp.zeros((LANES, SB_T), F32)
        state.extend(block(kb, vtb, c0, a0, qm[h], True))

    def body(n, st):
        j = i - 1 - n
        k0 = pl.multiple_of(j * SB_T, SB_T)
        kb = k_ref[0, pl.ds(k0, SB_T), :]
        vtb = vt_ref[j]
        out = []
        for h in range(HEAD_PAIR):
            out.extend(block(kb, vtb, st[2 * h], st[2 * h + 1], qm[h], False))
        return tuple(out)

    state = lax.fori_loop(0, i, body, tuple(state))

    row = lax.broadcasted_iota(jnp.int32, (LANES, SB_T), 0)
    acc_t = jnp.where(row < HEAD_DIM, state[1], state[3])
    o_ref[0] = acc_t.T


def _sb_attention(proj, batch, seq):
    p3 = proj.reshape(batch, seq, IN_COLS)
    n_pairs = SB_HEADS // HEAD_PAIR
    k_off = SB_W // LANES
    v_off = 2 * SB_W // LANES
    return pl.pallas_call(
        _sb_kernel,
        grid=(batch, n_pairs, seq // SB_T),
        in_specs=[
            pl.BlockSpec((1, SB_T, LANES), lambda b, p, i: (b, i, p)),
            pl.BlockSpec((1, seq, LANES), lambda b, p, i: (b, 0, k_off + p)),
            pl.BlockSpec((1, seq, LANES), lambda b, p, i: (b, 0, v_off + p)),
        ],
        out_specs=pl.BlockSpec((1, SB_T, LANES), lambda b, p, i: (b, i, p)),
        out_shape=jax.ShapeDtypeStruct((batch, seq, SB_W), F32),
        scratch_shapes=[pltpu.VMEM((seq // SB_T, LANES, SB_T), BF16)],
        compiler_params=_params(3),
        name="stick_breaking_attention",
    )(p3, p3, p3)


def _swa_kernel(sink_ref, q_ref, k_ref, v_ref, o_ref):
    i = pl.program_id(1)
    prev = jnp.maximum(i - 1, 0)
    p0 = pl.multiple_of(prev * SWA_T, SWA_T)
    c0 = pl.multiple_of(i * SWA_T, SWA_T)
    kp = k_ref[0, pl.ds(p0, SWA_T), :]
    kc = k_ref[0, pl.ds(c0, SWA_T), :]
    vp = v_ref[0, pl.ds(p0, SWA_T), :]
    vc = v_ref[0, pl.ds(c0, SWA_T), :]

    r = lax.broadcasted_iota(jnp.int32, (SWA_T, SWA_T), 0)
    c = lax.broadcasted_iota(jnp.int32, (SWA_T, SWA_T), 1)
    mask_p = jnp.logical_and(c > r, i > 0)
    mask_c = c <= r
    dist_p = (r + SWA_T - c).astype(F32)
    dist_c = (r - c).astype(F32)
    neg_inf = jnp.float32(-jnp.inf)

    for h in range(SWA_HEADS):
        g = h // SWA_GROUP
        slope = 2.0 ** (-8.0 * (h + 1.0) / SWA_HEADS)
        sink = sink_ref[h]
        hs = slice(h * HEAD_DIM, (h + 1) * HEAD_DIM)
        gs = slice(g * HEAD_DIM, (g + 1) * HEAD_DIM)
        qh = q_ref[0, :, hs] * Q_SCALE
        sp = jnp.where(mask_p, _dot_nt(qh, kp[:, gs]) - slope * dist_p, neg_inf)
        sc = jnp.where(mask_c, _dot_nt(qh, kc[:, gs]) - slope * dist_c, neg_inf)
        mx = jnp.maximum(jnp.max(sp, axis=-1, keepdims=True),
                         jnp.max(sc, axis=-1, keepdims=True))
        mx = jnp.maximum(mx, sink)
        pp = jnp.exp(sp - mx)
        pc = jnp.exp(sc - mx)
        den = (jnp.sum(pp, axis=-1, keepdims=True) + jnp.sum(pc, axis=-1, keepdims=True)
               + jnp.exp(sink - mx))
        o = _dot(pp.astype(BF16), vp[:, gs]) + _dot(pc.astype(BF16), vc[:, gs])
        o_ref[0, :, hs] = o / den


def _swa_attention(proj, sinks, batch, seq):
    p3 = proj.reshape(batch, seq, IN_COLS)
    q_blk = 3 * SB_W // SWA_Q_W
    k_blk = (3 * SB_W + SWA_Q_W) // SWA_KV_W
    v_blk = k_blk + 1
    return pl.pallas_call(
        _swa_kernel,
        grid=(batch, seq // SWA_T),
        in_specs=[
            pl.BlockSpec(memory_space=pltpu.SMEM),
            pl.BlockSpec((1, SWA_T, SWA_Q_W), lambda b, i: (b, i, q_blk)),
            pl.BlockSpec((1, seq, SWA_KV_W), lambda b, i: (b, 0, k_blk)),
            pl.BlockSpec((1, seq, SWA_KV_W), lambda b, i: (b, 0, v_blk)),
        ],
        out_specs=pl.BlockSpec((1, SWA_T, SWA_Q_W), lambda b, i: (b, i, 0)),
        out_shape=jax.ShapeDtypeStruct((batch, seq, SWA_Q_W), F32),
        compiler_params=_params(2),
        name="sliding_window_attention",
    )(sinks, p3, p3, p3)


def _out_proj_kernel(sb_ref, sw_ref, ga_ref, gb_ref, w_ref, x_ref, o_ref, m_ref):
    @pl.when(pl.program_id(1) == 0)
    def _():
        m_ref[:, :SB_W] = _rms(sb_ref[...], ga_ref[...]).astype(BF16)
        m_ref[:, SB_W:] = _rms(sw_ref[...], gb_ref[...]).astype(BF16)

    o_ref[...] = x_ref[...] + _dot(m_ref[...], w_ref[...])


def _out_proj(sb_o, sw_o, ga, gb, w, x):
    t, d = x.shape
    mix = SB_W + SWA_Q_W
    return pl.pallas_call(
        _out_proj_kernel,
        grid=(t // OUT_TM, d // OUT_TN),
        in_specs=[
            pl.BlockSpec((OUT_TM, SB_W), lambda i, j: (i, 0)),
            pl.BlockSpec((OUT_TM, SWA_Q_W), lambda i, j: (i, 0)),
            pl.BlockSpec((1, SB_W), lambda i, j: (0, 0)),
            pl.BlockSpec((1, SWA_Q_W), lambda i, j: (0, 0)),
            pl.BlockSpec((mix, OUT_TN), lambda i, j: (0, j)),
            pl.BlockSpec((OUT_TM, OUT_TN), lambda i, j: (i, j)),
        ],
        out_specs=pl.BlockSpec((OUT_TM, OUT_TN), lambda i, j: (i, j)),
        out_shape=jax.ShapeDtypeStruct((t, d), F32),
        scratch_shapes=[pltpu.VMEM((OUT_TM, mix), BF16)],
        compiler_params=_params(2),
        name="out_proj",
    )(sb_o, sw_o, ga.reshape(1, SB_W), gb.reshape(1, SWA_Q_W), w, x)


def _ffn_kernel(x_ref, g_ref, wg_ref, wu_ref, wd_ref, gf_ref, o_ref, h_ref, acc_ref, *, final_norm):
    f = pl.program_id(1)

    @pl.when(f == 0)
    def _():
        h_ref[...] = _rms(x_ref[...], g_ref[...]).astype(BF16)
        acc_ref[...] = jnp.zeros_like(acc_ref)

    h = h_ref[...]
    gate = _dot(h, wg_ref[...])
    up = _dot(h, wu_ref[...])
    act = (gate * jax.nn.sigmoid(gate) * up).astype(BF16)
    acc_ref[...] += _dot(act, wd_ref[...])

    @pl.when(f == pl.num_programs(1) - 1)
    def _():
        y = x_ref[...] + acc_ref[...]
        if final_norm:
            y = _rms(y, gf_ref[...])
        o_ref[...] = y


def _ffn(x, g, w_gate_up, w_down, g_final, final_norm):
    t, d = x.shape
    n_f = D_FF // FFN_TF
    return pl.pallas_call(
        functools.partial(_ffn_kernel, final_norm=final_norm),
        grid=(t // FFN_TM, n_f),
        in_specs=[
            pl.BlockSpec((FFN_TM, d), lambda i, f: (i, 0)),
            pl.BlockSpec((1, d), lambda i, f: (0, 0)),
            pl.BlockSpec((d, FFN_TF), lambda i, f: (0, f)),
            pl.BlockSpec((d, FFN_TF), lambda i, f: (0, f + n_f)),
            pl.BlockSpec((FFN_TF, d), lambda i, f: (f, 0)),
            pl.BlockSpec((1, d), lambda i, f: (0, 0)),
        ],
        out_specs=pl.BlockSpec((FFN_TM, d), lambda i, f: (i, 0)),
        out_shape=jax.ShapeDtypeStruct((t, d), F32),
        scratch_shapes=[pltpu.VMEM((FFN_TM, d), BF16), pltpu.VMEM((FFN_TM, d), F32)],
        compiler_params=_params(2),
        name="swiglu_ffn",
    )(x, g.reshape(1, d), w_gate_up, w_gate_up, w_down, g_final.reshape(1, d))


def kernel(x, ln_mix, w_in, sb_out_norm, swa_out_norm, swa_sinks, w_out, ln_ffn, w_gate_up, w_down, ln_final):
    batch, seq, d = x.shape
    depth = w_in.shape[0]
    xt = x.reshape(batch * seq, d)
    for l in range(depth):
        proj = _norm_proj(xt, ln_mix[l], w_in[l].astype(BF16))
        sb_o = _sb_attention(proj, batch, seq).reshape(batch * seq, SB_W)
        sw_o = _swa_attention(proj, swa_sinks[l], batch, seq).reshape(batch * seq, SWA_Q_W)
        xt = _out_proj(sb_o, sw_o, sb_out_norm[l], swa_out_norm[l], w_out[l].astype(BF16), xt)
        xt = _ffn(xt, ln_ffn[l], w_gate_up[l].astype(BF16), w_down[l].astype(BF16),
                  ln_final, final_norm=(l == depth - 1))
    return xt.reshape(batch, seq, d)
```

```python
import functools

import jax
import jax.numpy as jnp
from jax import lax
from jax.experimental import pallas as pl
from jax.experimental.pallas import tpu as pltpu

D_MODEL = 2048
HEAD_DIM = 64
SB_HEADS = 16
SWA_HEADS = 16
SWA_KV_HEADS = 4
SWA_GROUP = SWA_HEADS // SWA_KV_HEADS
WINDOW = 128
D_FF = 5632
RMS_EPS = 1e-5

SB_W = SB_HEADS * HEAD_DIM
SWA_Q_W = SWA_HEADS * HEAD_DIM
SWA_KV_W = SWA_KV_HEADS * HEAD_DIM
IN_COLS = 3 * SB_W + SWA_Q_W + 2 * SWA_KV_W
Q_SCALE = HEAD_DIM ** -0.5

LANES = 128
HEAD_PAIR = LANES // HEAD_DIM
VMEM_LIMIT = 56 * 1024 * 1024

F32 = jnp.float32
BF16 = jnp.bfloat16

PROJ_TM, PROJ_TN = 1024, 768
SB_T = 256
SB_SUB = 128
SWA_T = WINDOW
OUT_TM = 512
FFN_TM, FFN_TF = 512, 512


def _params(n_axes):
    return pltpu.CompilerParams(
        dimension_semantics=("arbitrary",) * n_axes,
        vmem_limit_bytes=VMEM_LIMIT)


def _rms(x, g):
    ms = jnp.mean(x * x, axis=-1, keepdims=True)
    return x * lax.rsqrt(ms + RMS_EPS) * g


def _dot(a, b):
    return jnp.dot(a, b, preferred_element_type=F32)


def _dot_nt(a, b):
    return lax.dot_general(a, b, (((1,), (1,)), ((), ())),
                           preferred_element_type=F32)


def _norm_proj_kernel(x_ref, g_ref, w_ref, o_ref, h_ref):
    @pl.when(pl.program_id(1) == 0)
    def _():
        h_ref[...] = _rms(x_ref[...], g_ref[...]).astype(BF16)

    o_ref[...] = _dot(h_ref[...], w_ref[...]).astype(o_ref.dtype)


def _norm_proj(x, g, w):
    t, d = x.shape
    n = w.shape[1]
    return pl.pallas_call(
        _norm_proj_kernel,
        grid=(t // PROJ_TM, n // PROJ_TN),
        in_specs=[
            pl.BlockSpec((PROJ_TM, d), lambda i, j: (i, 0)),
            pl.BlockSpec((1, d), lambda i, j: (0, 0)),
            pl.BlockSpec((d, PROJ_TN), lambda i, j: (0, j)),
        ],
        out_specs=pl.BlockSpec((PROJ_TM, PROJ_TN), lambda i, j: (i, j)),
        out_shape=jax.ShapeDtypeStruct((t, n), BF16),
        scratch_shapes=[pltpu.VMEM((PROJ_TM, d), BF16)],
        compiler_params=_params(2),
        name="norm_in_proj",
    )(x, g.reshape(1, d), w)


def _sb_kernel(q_ref, k_ref, v_ref, o_ref, vt_ref, z_ref, w_ref, acc_ref):
    i = pl.program_id(2)
    n_kblk = vt_ref.shape[0]

    @pl.when(i == 0)
    def _():
        def tr(c, _):
            r0 = pl.multiple_of(c * SB_T, SB_T)
            vt_ref[c] = v_ref[0, pl.ds(r0, SB_T), :].astype(F32).T.astype(BF16)
            return 0
        lax.fori_loop(0, n_kblk, tr, 0)

    q = q_ref[0] * Q_SCALE
    lane = lax.broadcasted_iota(jnp.int32, q.shape, 1)
    zero = jnp.zeros_like(q)
    qm = [jnp.where((lane // HEAD_DIM) == h, q, zero) for h in range(HEAD_PAIR)]

    r = lax.broadcasted_iota(jnp.int32, (SB_SUB, 2 * SB_SUB), 0)
    j2 = lax.broadcasted_iota(jnp.int32, (SB_SUB, 2 * SB_SUB), 1)
    tri = jnp.where((j2 % SB_SUB) > r, 1.0, 0.0).astype(BF16)

    srow = lax.broadcasted_iota(jnp.int32, (SB_T, SB_T), 0)
    tcol = lax.broadcasted_iota(jnp.int32, (SB_T, SB_T), 1)
    causal = srow < tcol

    n_sub = SB_T // SB_SUB
    heads = range(HEAD_PAIR)

    def scores(j, h):
        k0 = pl.multiple_of(j * SB_T, SB_T)
        z_ref[h] = _dot_nt(k_ref[0, pl.ds(k0, SB_T), :], qm[h])

    def values(j, h):
        acc_ref[h] += _dot(vt_ref[j], w_ref[h])

    def weights(h, c, diag):
        zt = z_ref[h]
        l = jnp.log(1.0 + jnp.exp(-jnp.abs(zt)))
        ls = jnp.minimum(zt, 0.0) - l
        lom = ls - zt
        if diag:
            lom = jnp.where(causal, lom, 0.0)
        hi = lom.astype(BF16)
        lo = (lom - hi.astype(F32)).astype(BF16)
        survs = [None] * n_sub
        for sb in reversed(range(n_sub)):
            r0, r1 = sb * SB_SUB, (sb + 1) * SB_SUB
            cs = _dot(tri, jnp.concatenate([hi[r0:r1], lo[r0:r1]], axis=0))
            survs[sb] = cs + c
            c = c + (cs[0:1, :] + lom[r0:r0 + 1, :])
        w = jnp.exp(ls + jnp.concatenate(survs, axis=0))
        if diag:
            w = jnp.where(causal, w, 0.0)
        w_ref[h] = w.astype(BF16)
        return c

    acc_ref[...] = jnp.zeros_like(acc_ref)
    for h in heads:
        scores(i, h)
    cs = tuple(weights(h, jnp.zeros((1, SB_T), F32), True) for h in heads)
    for h in heads:
        scores(jnp.maximum(i - 1, 0), h)

    def body(n, cs):
        j = i - 1 - n
        for h in heads:
            values(j + 1, h)
        cs = tuple(weights(h, cs[h], False) for h in heads)
        for h in heads:
            scores(jnp.maximum(j - 1, 0), h)
        return cs

    lax.fori_loop(0, i, body, cs)
    for h in heads:
        values(0, h)

    row = lax.broadcasted_iota(jnp.int32, (LANES, SB_T), 0)
    acc_t = jnp.where(row < HEAD_DIM, acc_ref[0], acc_ref[1])
    o_ref[0] = acc_t.T


def _sb_attention(proj, batch, seq):
    p3 = proj.reshape(batch, seq, IN_COLS)
    n_pairs = SB_HEADS // HEAD_PAIR
    k_off = SB_W // LANES
    v_off = 2 * SB_W // LANES
    return pl.pallas_call(
        _sb_kernel,
        grid=(batch, n_pairs, seq // SB_T),
        in_specs=[
            pl.BlockSpec((1, SB_T, LANES), lambda b, p, i: (b, i, p)),
            pl.BlockSpec((1, seq, LANES), lambda b, p, i: (b, 0, k_off + p)),
            pl.BlockSpec((1, seq, LANES), lambda b, p, i: (b, 0, v_off + p)),
        ],
        out_specs=pl.BlockSpec((1, SB_T, LANES), lambda b, p, i: (b, i, p)),
        out_shape=jax.ShapeDtypeStruct((batch, seq, SB_W), F32),
        scratch_shapes=[
            pltpu.VMEM((seq // SB_T, LANES, SB_T), BF16),
            pltpu.VMEM((HEAD_PAIR, SB_T, SB_T), F32),
            pltpu.VMEM((HEAD_PAIR, SB_T, SB_T), BF16),
            pltpu.VMEM((HEAD_PAIR, LANES, SB_T), F32),
        ],
        compiler_params=_params(3),
        name="stick_breaking_attention",
    )(p3, p3, p3)


def _swa_kernel(sink_ref, q_ref, k_ref, v_ref, o_ref, vt_ref):
    i = pl.program_id(1)
    n_blk = vt_ref.shape[0]

    @pl.when(i == 0)
    def _():
        def tr(c, _):
            r0 = pl.multiple_of(c * SWA_T, SWA_T)
            vt_ref[c] = v_ref[0, pl.ds(r0, SWA_T), :].astype(F32).T.astype(BF16)
            return 0
        lax.fori_loop(0, n_blk, tr, 0)

    prev = jnp.maximum(i - 1, 0)
    p0 = pl.multiple_of(prev * SWA_T, SWA_T)
    c0 = pl.multiple_of(i * SWA_T, SWA_T)
    kband = jnp.concatenate([k_ref[0, pl.ds(p0, SWA_T), :],
                             k_ref[0, pl.ds(c0, SWA_T), :]], axis=0)
    vtband = jnp.concatenate([vt_ref[prev], vt_ref[i]], axis=1)

    band = 2 * SWA_T
    c = lax.broadcasted_iota(jnp.int32, (band, SWA_T), 0)
    r = lax.broadcasted_iota(jnp.int32, (band, SWA_T), 1)
    dist = r + SWA_T - c
    valid = (dist >= 0) & (dist < WINDOW) & ((c >= SWA_T) | (i > 0))
    dist_f = dist.astype(F32)
    neg_inf = jnp.float32(-jnp.inf)

    o_heads = []
    for g in range(SWA_KV_HEADS):
        gs = slice(g * HEAD_DIM, (g + 1) * HEAD_DIM)
        group = range(g * SWA_GROUP, (g + 1) * SWA_GROUP)
        qg = jnp.concatenate(
            [q_ref[0, :, h * HEAD_DIM:(h + 1) * HEAD_DIM] for h in group], axis=0) * Q_SCALE
        zt = _dot_nt(kband[:, gs], qg)
        ps, dens = [], []
        for n, h in enumerate(group):
            slope = 2.0 ** (-8.0 * (h + 1.0) / SWA_HEADS)
            sink = sink_ref[h]
            s = jnp.where(valid, zt[:, n * SWA_T:(n + 1) * SWA_T] - slope * dist_f, neg_inf)
            mx = jnp.maximum(jnp.max(s, axis=0, keepdims=True), sink)
            p = jnp.exp(s - mx)
            dens.append(jnp.sum(p, axis=0, keepdims=True) + jnp.exp(sink - mx))
            ps.append(p.astype(BF16))
        o_t = _dot(vtband[gs, :], jnp.concatenate(ps, axis=1))
        for n in range(SWA_GROUP):
            o_heads.append(o_t[:, n * SWA_T:(n + 1) * SWA_T] / dens[n])

    for m in range(SWA_HEADS // HEAD_PAIR):
        pair = jnp.concatenate(o_heads[HEAD_PAIR * m:HEAD_PAIR * (m + 1)], axis=0)
        o_ref[0, :, m * LANES:(m + 1) * LANES] = pair.T


def _swa_attention(proj, sinks, batch, seq):
    p3 = proj.reshape(batch, seq, IN_COLS)
    q_blk = 3 * SB_W // SWA_Q_W
    k_blk = (3 * SB_W + SWA_Q_W) // SWA_KV_W
    v_blk = k_blk + 1
    return pl.pallas_call(
        _swa_kernel,
        grid=(batch, seq // SWA_T),
        in_specs=[
            pl.BlockSpec(memory_space=pltpu.SMEM),
            pl.BlockSpec((1, SWA_T, SWA_Q_W), lambda b, i: (b, i, q_blk)),
            pl.BlockSpec((1, seq, SWA_KV_W), lambda b, i: (b, 0, k_blk)),
            pl.BlockSpec((1, seq, SWA_KV_W), lambda b, i: (b, 0, v_blk)),
        ],
        out_specs=pl.BlockSpec((1, SWA_T, SWA_Q_W), lambda b, i: (b, i, 0)),
        out_shape=jax.ShapeDtypeStruct((batch, seq, SWA_Q_W), F32),
        scratch_shapes=[pltpu.VMEM((seq // SWA_T, SWA_KV_W, SWA_T), BF16)],
        compiler_params=_params(2),
        name="sliding_window_attention",
    )(sinks, p3, p3, p3)


def _out_proj_kernel(sb_ref, sw_ref, ga_ref, gb_ref, w_ref, x_ref, o_ref):
    mixed = jnp.concatenate([_rms(sb_ref[...], ga_ref[...]).astype(BF16),
                             _rms(sw_ref[...], gb_ref[...]).astype(BF16)], axis=1)
    o_ref[...] = x_ref[...] + _dot(mixed, w_ref[...])


def _out_proj(sb_o, sw_o, ga, gb, w, x):
    t, d = x.shape
    mix = SB_W + SWA_Q_W
    return pl.pallas_call(
        _out_proj_kernel,
        grid=(t // OUT_TM,),
        in_specs=[
            pl.BlockSpec((OUT_TM, SB_W), lambda i: (i, 0)),
            pl.BlockSpec((OUT_TM, SWA_Q_W), lambda i: (i, 0)),
            pl.BlockSpec((1, SB_W), lambda i: (0, 0)),
            pl.BlockSpec((1, SWA_Q_W), lambda i: (0, 0)),
            pl.BlockSpec((mix, d), lambda i: (0, 0)),
            pl.BlockSpec((OUT_TM, d), lambda i: (i, 0)),
        ],
        out_specs=pl.BlockSpec((OUT_TM, d), lambda i: (i, 0)),
        out_shape=jax.ShapeDtypeStruct((t, d), F32),
        compiler_params=_params(1),
        name="out_proj",
    )(sb_o, sw_o, ga.reshape(1, SB_W), gb.reshape(1, SWA_Q_W), w, x)


def _ffn_kernel(x_ref, g_ref, wg_ref, wu_ref, wd_ref, gf_ref, o_ref, h_ref, acc_ref, *, final_norm):
    f = pl.program_id(1)

    @pl.when(f == 0)
    def _():
        h_ref[...] = _rms(x_ref[...], g_ref[...]).astype(BF16)
        acc_ref[...] = jnp.zeros_like(acc_ref)

    h = h_ref[...]
    gate = _dot(h, wg_ref[...])
    up = _dot(h, wu_ref[...])
    act = (gate * jax.nn.sigmoid(gate) * up).astype(BF16)
    acc_ref[...] += _dot(act, wd_ref[...])

    @pl.when(f == pl.num_programs(1) - 1)
    def _():
        y = x_ref[...] + acc_ref[...]
        if final_norm:
            y = _rms(y, gf_ref[...])
        o_ref[...] = y


def _ffn(x, g, w_gate_up, w_down, g_final, final_norm):
    t, d = x.shape
    n_f = D_FF // FFN_TF
    return pl.pallas_call(
        functools.partial(_ffn_kernel, final_norm=final_norm),
        grid=(t // FFN_TM, n_f),
        in_specs=[
            pl.BlockSpec((FFN_TM, d), lambda i, f: (i, 0)),
            pl.BlockSpec((1, d), lambda i, f: (0, 0)),
            pl.BlockSpec((d, FFN_TF), lambda i, f: (0, f)),
            pl.BlockSpec((d, FFN_TF), lambda i, f: (0, f + n_f)),
            pl.BlockSpec((FFN_TF, d), lambda i, f: (f, 0)),
            pl.BlockSpec((1, d), lambda i, f: (0, 0)),
        ],
        out_specs=pl.BlockSpec((FFN_TM, d), lambda i, f: (i, 0)),
        out_shape=jax.ShapeDtypeStruct((t, d), F32),
        scratch_shapes=[pltpu.VMEM((FFN_TM, d), BF16), pltpu.VMEM((FFN_TM, d), F32)],
        compiler_params=_params(2),
        name="swiglu_ffn",
    )(x, g.reshape(1, d), w_gate_up, w_gate_up, w_down, g_final.reshape(1, d))


def kernel(x, ln_mix, w_in, sb_out_norm, swa_out_norm, swa_sinks, w_out, ln_ffn, w_gate_up, w_down, ln_final):
    batch, seq, d = x.shape
    depth = w_in.shape[0]
    xt = x.reshape(batch * seq, d)
    for l in range(depth):
        proj = _norm_proj(xt, ln_mix[l], w_in[l].astype(BF16))
        sb_o = _sb_attention(proj, batch, seq).reshape(batch * seq, SB_W)
        sw_o = _swa_attention(proj, swa_sinks[l], batch, seq).reshape(batch * seq, SWA_Q_W)
        xt = _out_proj(sb_o, sw_o, sb_out_norm[l], swa_out_norm[l], w_out[l].astype(BF16), xt)
        xt = _ffn(xt, ln_ffn[l], w_gate_up[l].astype(BF16), w_down[l].astype(BF16),
                  ln_final, final_norm=(l == depth - 1))
    return xt.reshape(batch, seq, d)
```

```python
import functools

import jax
import jax.numpy as jnp
from jax import lax
from jax.experimental import pallas as pl
from jax.experimental.pallas import tpu as pltpu

D_MODEL = 2048
HEAD_DIM = 64
SB_HEADS = 16
SWA_HEADS = 16
SWA_KV_HEADS = 4
SWA_GROUP = SWA_HEADS // SWA_KV_HEADS
WINDOW = 128
D_FF = 5632
RMS_EPS = 1e-5

SB_W = SB_HEADS * HEAD_DIM
SWA_Q_W = SWA_HEADS * HEAD_DIM
SWA_KV_W = SWA_KV_HEADS * HEAD_DIM
IN_COLS = 3 * SB_W + SWA_Q_W + 2 * SWA_KV_W
Q_SCALE = HEAD_DIM ** -0.5
LOG2E = 1.4426950408889634

LANES = 128
HEAD_PAIR = LANES // HEAD_DIM
VMEM_LIMIT = 56 * 1024 * 1024

F32 = jnp.float32
BF16 = jnp.bfloat16

PROJ_TM, PROJ_TN = 1024, 768
SB_T = 256
SB_SUB = 128
SWA_T = WINDOW
OUT_TM = 512
FFN_TM, FFN_TF = 512, 512


def _params(n_axes):
    return pltpu.CompilerParams(
        dimension_semantics=("arbitrary",) * n_axes,
        vmem_limit_bytes=VMEM_LIMIT)


def _rms(x, g):
    ms = jnp.mean(x * x, axis=-1, keepdims=True)
    return x * lax.rsqrt(ms + RMS_EPS) * g


def _dot(a, b):
    return jnp.dot(a, b, preferred_element_type=F32)


def _dot_nt(a, b):
    return lax.dot_general(a, b, (((1,), (1,)), ((), ())),
                           preferred_element_type=F32)


def _norm_proj_kernel(x_ref, g_ref, w_ref, o_ref, h_ref):
    @pl.when(pl.program_id(1) == 0)
    def _():
        h_ref[...] = _rms(x_ref[...], g_ref[...]).astype(BF16)

    res = _dot(h_ref[...], w_ref[...])
    for c in range(PROJ_TN // LANES):
        o_ref[0, c] = res[:, c * LANES:(c + 1) * LANES].astype(o_ref.dtype)


def _norm_proj(x, g, w, batch, seq):
    t, d = x.shape
    n = w.shape[1]
    tiles_per_seq = seq // PROJ_TM
    blk_per_tile = PROJ_TN // LANES
    return pl.pallas_call(
        _norm_proj_kernel,
        grid=(t // PROJ_TM, n // PROJ_TN),
        in_specs=[
            pl.BlockSpec((PROJ_TM, d), lambda i, j: (i, 0)),
            pl.BlockSpec((1, d), lambda i, j: (0, 0)),
            pl.BlockSpec((d, PROJ_TN), lambda i, j: (0, j)),
        ],
        out_specs=pl.BlockSpec((1, blk_per_tile, PROJ_TM, LANES),
                               lambda i, j: (i // tiles_per_seq, j, i % tiles_per_seq, 0)),
        out_shape=jax.ShapeDtypeStruct((batch, n // LANES, seq, LANES), BF16),
        scratch_shapes=[pltpu.VMEM((PROJ_TM, d), BF16)],
        compiler_params=_params(2),
        name="norm_in_proj",
    )(x, g.reshape(1, d), w)


def _sb_kernel(q_ref, k_ref, v_ref, o_ref, vt_ref, z_ref, w_ref, acc_ref):
    i = pl.program_id(2)
    n_kblk = vt_ref.shape[0]

    @pl.when(i == 0)
    def _():
        def tr(c, _):
            r0 = pl.multiple_of(c * SB_T, SB_T)
            vt_ref[c] = v_ref[0, 0, pl.ds(r0, SB_T), :].astype(F32).T.astype(BF16)
            return 0
        lax.fori_loop(0, n_kblk, tr, 0)

    q = q_ref[0, 0] * Q_SCALE
    lane = lax.broadcasted_iota(jnp.int32, q.shape, 1)
    zero = jnp.zeros_like(q)
    qm = [jnp.where((lane // HEAD_DIM) == h, q, zero) for h in range(HEAD_PAIR)]

    r = lax.broadcasted_iota(jnp.int32, (SB_SUB, 2 * SB_SUB), 0)
    j2 = lax.broadcasted_iota(jnp.int32, (SB_SUB, 2 * SB_SUB), 1)
    tri = jnp.where((j2 % SB_SUB) >= r, 1.0, 0.0).astype(BF16)

    srow = lax.broadcasted_iota(jnp.int32, (SB_T, SB_T), 0)
    tcol = lax.broadcasted_iota(jnp.int32, (SB_T, SB_T), 1)
    causal = srow < tcol

    n_sub = SB_T // SB_SUB
    heads = range(HEAD_PAIR)

    def scores(k, slot, h):
        j = jnp.maximum(i - k, 0)
        k0 = pl.multiple_of(j * SB_T, SB_T)
        z_ref[slot, h] = _dot_nt(k_ref[0, 0, pl.ds(k0, SB_T), :], qm[h])

    def values(k, h):
        acc_ref[h] += _dot(vt_ref[i - k], w_ref[h])

    def weights(slot, h, c, diag):
        zt = z_ref[slot, h]
        sp = jnp.maximum(zt, 0.0) + jnp.log(1.0 + jnp.exp2(jnp.abs(zt) * -LOG2E))
        if diag:
            sp = jnp.where(causal, sp, 0.0)
        hi = sp.astype(BF16)
        lo = (sp - hi.astype(F32)).astype(BF16)
        xs = [None] * n_sub
        for sb in reversed(range(n_sub)):
            r0, r1 = sb * SB_SUB, (sb + 1) * SB_SUB
            cs = _dot(tri, jnp.concatenate([hi[r0:r1], lo[r0:r1]], axis=0))
            xs[sb] = (zt[r0:r1] - cs) - c
            c = c + cs[0:1, :]
        w = jnp.exp(jnp.concatenate(xs, axis=0))
        if diag:
            w = jnp.where(causal, w, 0.0)
        w_ref[h] = w.astype(BF16)
        return c

    acc_ref[...] = jnp.zeros_like(acc_ref)
    for h in heads:
        scores(0, 0, h)
    for h in heads:
        scores(1, 1, h)
    cs = tuple(weights(0, h, jnp.zeros((1, SB_T), F32), True) for h in heads)
    for h in heads:
        scores(2, 0, h)

    def pair(m, cs):
        k = 2 * m + 1
        for h in heads:
            values(k - 1, h)
        cs = tuple(weights(1, h, cs[h], False) for h in heads)
        for h in heads:
            scores(k + 2, 1, h)
        for h in heads:
            values(k, h)
        cs = tuple(weights(0, h, cs[h], False) for h in heads)
        for h in heads:
            scores(k + 3, 0, h)
        return cs

    cs = lax.fori_loop(0, lax.shift_right_logical(i, 1), pair, cs)

    @pl.when((i & 1) == 1)
    def _():
        for h in heads:
            values(i - 1, h)
        for h in heads:
            weights(1, h, cs[h], False)

    for h in heads:
        values(i, h)

    row = lax.broadcasted_iota(jnp.int32, (LANES, SB_T), 0)
    acc_t = jnp.where(row < HEAD_DIM, acc_ref[0], acc_ref[1])
    o_ref[0] = acc_t.T


def _sb_attention(proj, batch, seq):
    n_pairs = SB_HEADS // HEAD_PAIR
    k_off = SB_W // LANES
    v_off = 2 * SB_W // LANES
    return pl.pallas_call(
        _sb_kernel,
        grid=(batch, n_pairs, seq // SB_T),
        in_specs=[
            pl.BlockSpec((1, 1, SB_T, LANES), lambda b, p, i: (b, p, i, 0)),
            pl.BlockSpec((1, 1, seq, LANES), lambda b, p, i: (b, k_off + p, 0, 0)),
            pl.BlockSpec((1, 1, seq, LANES), lambda b, p, i: (b, v_off + p, 0, 0)),
        ],
        out_specs=pl.BlockSpec((1, SB_T, LANES), lambda b, p, i: (b, i, p)),
        out_shape=jax.ShapeDtypeStruct((batch, seq, SB_W), F32),
        scratch_shapes=[
            pltpu.VMEM((seq // SB_T, LANES, SB_T), BF16),
            pltpu.VMEM((2, HEAD_PAIR, SB_T, SB_T), F32),
            pltpu.VMEM((HEAD_PAIR, SB_T, SB_T), BF16),
            pltpu.VMEM((HEAD_PAIR, LANES, SB_T), F32),
        ],
        compiler_params=_params(3),
        name="stick_breaking_attention",
    )(proj, proj, proj)


def _swa_kernel(sink_ref, q_ref, k_ref, v_ref, o_ref, vt_ref):
    i = pl.program_id(1)
    n_blk = vt_ref.shape[0]

    @pl.when(i == 0)
    def _():
        def tr(c, _):
            r0 = pl.multiple_of(c * SWA_T, SWA_T)
            for m in range(SWA_KV_W // LANES):
                vt_ref[c, m * LANES:(m + 1) * LANES, :] = (
                    v_ref[0, m, pl.ds(r0, SWA_T), :].astype(F32).T.astype(BF16))
            return 0
        lax.fori_loop(0, n_blk, tr, 0)

    prev = jnp.maximum(i - 1, 0)
    p0 = pl.multiple_of(prev * SWA_T, SWA_T)
    c0 = pl.multiple_of(i * SWA_T, SWA_T)
    kband = [jnp.concatenate([k_ref[0, m, pl.ds(p0, SWA_T), :],
                              k_ref[0, m, pl.ds(c0, SWA_T), :]], axis=0)
             for m in range(SWA_KV_W // LANES)]
    vtband = jnp.concatenate([vt_ref[prev], vt_ref[i]], axis=1)

    band = 2 * SWA_T
    c = lax.broadcasted_iota(jnp.int32, (band, SWA_T), 0)
    r = lax.broadcasted_iota(jnp.int32, (band, SWA_T), 1)
    dist = r + SWA_T - c
    valid = (dist >= 0) & (dist < WINDOW) & ((c >= SWA_T) | (i > 0))
    dist_f = dist.astype(F32)
    neg_inf = jnp.float32(-jnp.inf)

    o_heads = []
    for g in range(SWA_KV_HEADS):
        gs = slice(g * HEAD_DIM, (g + 1) * HEAD_DIM)
        group = range(g * SWA_GROUP, (g + 1) * SWA_GROUP)
        half = lambda n: slice((n % HEAD_PAIR) * HEAD_DIM, (n % HEAD_PAIR + 1) * HEAD_DIM)
        qg = jnp.concatenate(
            [q_ref[0, h // HEAD_PAIR, :, half(h)] for h in group], axis=0) * Q_SCALE
        zt = _dot_nt(kband[g // HEAD_PAIR][:, half(g)], qg)
        ps, dens = [], []
        for n, h in enumerate(group):
            slope = 2.0 ** (-8.0 * (h + 1.0) / SWA_HEADS)
            sink = sink_ref[h]
            s = jnp.where(valid, zt[:, n * SWA_T:(n + 1) * SWA_T] - slope * dist_f, neg_inf)
            mx = jnp.maximum(jnp.max(s, axis=0, keepdims=True), sink)
            p = jnp.exp(s - mx)
            dens.append(jnp.sum(p, axis=0, keepdims=True) + jnp.exp(sink - mx))
            ps.append(p.astype(BF16))
        o_t = _dot(vtband[gs, :], jnp.concatenate(ps, axis=1))
        for n in range(SWA_GROUP):
            o_heads.append(o_t[:, n * SWA_T:(n + 1) * SWA_T] / dens[n])

    for m in range(SWA_HEADS // HEAD_PAIR):
        pair = jnp.concatenate(o_heads[HEAD_PAIR * m:HEAD_PAIR * (m + 1)], axis=0)
        o_ref[0, :, m * LANES:(m + 1) * LANES] = pair.T


def _swa_attention(proj, sinks, batch, seq):
    q_n, kv_n = SWA_Q_W // LANES, SWA_KV_W // LANES
    q_blk = 3 * SB_W // SWA_Q_W
    k_blk = (3 * SB_W + SWA_Q_W) // SWA_KV_W
    v_blk = k_blk + 1
    return pl.pallas_call(
        _swa_kernel,
        grid=(batch, seq // SWA_T),
        in_specs=[
            pl.BlockSpec(memory_space=pltpu.SMEM),
            pl.BlockSpec((1, q_n, SWA_T, LANES), lambda b, i: (b, q_blk, i, 0)),
            pl.BlockSpec((1, kv_n, seq, LANES), lambda b, i: (b, k_blk, 0, 0)),
            pl.BlockSpec((1, kv_n, seq, LANES), lambda b, i: (b, v_blk, 0, 0)),
        ],
        out_specs=pl.BlockSpec((1, SWA_T, SWA_Q_W), lambda b, i: (b, i, 0)),
        out_shape=jax.ShapeDtypeStruct((batch, seq, SWA_Q_W), F32),
        scratch_shapes=[pltpu.VMEM((seq // SWA_T, SWA_KV_W, SWA_T), BF16)],
        compiler_params=_params(2),
        name="sliding_window_attention",
    )(sinks, proj, proj, proj)


def _out_proj_kernel(sb_ref, sw_ref, ga_ref, gb_ref, w_ref, x_ref, o_ref):
    mixed = jnp.concatenate([_rms(sb_ref[...], ga_ref[...]).astype(BF16),
                             _rms(sw_ref[...], gb_ref[...]).astype(BF16)], axis=1)
    o_ref[...] = x_ref[...] + _dot(mixed, w_ref[...])


def _out_proj(sb_o, sw_o, ga, gb, w, x):
    t, d = x.shape
    mix = SB_W + SWA_Q_W
    return pl.pallas_call(
        _out_proj_kernel,
        grid=(t // OUT_TM,),
        in_specs=[
            pl.BlockSpec((OUT_TM, SB_W), lambda i: (i, 0)),
            pl.BlockSpec((OUT_TM, SWA_Q_W), lambda i: (i, 0)),
            pl.BlockSpec((1, SB_W), lambda i: (0, 0)),
            pl.BlockSpec((1, SWA_Q_W), lambda i: (0, 0)),
            pl.BlockSpec((mix, d), lambda i: (0, 0)),
            pl.BlockSpec((OUT_TM, d), lambda i: (i, 0)),
        ],
        out_specs=pl.BlockSpec((OUT_TM, d), lambda i: (i, 0)),
        out_shape=jax.ShapeDtypeStruct((t, d), F32),
        compiler_params=_params(1),
        name="out_proj",
    )(sb_o, sw_o, ga.reshape(1, SB_W), gb.reshape(1, SWA_Q_W), w, x)


def _ffn_kernel(x_ref, g_ref, wg_ref, wu_ref, wd_ref, gf_ref, o_ref, h_ref, acc_ref, *, final_norm):
    f = pl.program_id(1)

    @pl.when(f == 0)
    def _():
        h_ref[...] = _rms(x_ref[...], g_ref[...]).astype(BF16)
        acc_ref[...] = jnp.zeros_like(acc_ref)

    h = h_ref[...]
    gate = _dot(h, wg_ref[...])
    up = _dot(h, wu_ref[...])
    act = (gate * jax.nn.sigmoid(gate) * up).astype(BF16)
    acc_ref[...] += _dot(act, wd_ref[...])

    @pl.when(f == pl.num_programs(1) - 1)
    def _():
        y = x_ref[...] + acc_ref[...]
        if final_norm:
            y = _rms(y, gf_ref[...])
        o_ref[...] = y


def _ffn(x, g, w_gate_up, w_down, g_final, final_norm):
    t, d = x.shape
    n_f = D_FF // FFN_TF
    return pl.pallas_call(
        functools.partial(_ffn_kernel, final_norm=final_norm),
        grid=(t // FFN_TM, n_f),
        in_specs=[
            pl.BlockSpec((FFN_TM, d), lambda i, f: (i, 0)),
            pl.BlockSpec((1, d), lambda i, f: (0, 0)),
            pl.BlockSpec((d, FFN_TF), lambda i, f: (0, f)),
            pl.BlockSpec((d, FFN_TF), lambda i, f: (0, f + n_f)),
            pl.BlockSpec((FFN_TF, d), lambda i, f: (f, 0)),
            pl.BlockSpec((1, d), lambda i, f: (0, 0)),
        ],
        out_specs=pl.BlockSpec((FFN_TM, d), lambda i, f: (i, 0)),
        out_shape=jax.ShapeDtypeStruct((t, d), F32),
        scratch_shapes=[pltpu.VMEM((FFN_TM, d), BF16), pltpu.VMEM((FFN_TM, d), F32)],
        compiler_params=_params(2),
        name="swiglu_ffn",
    )(x, g.reshape(1, d), w_gate_up, w_gate_up, w_down, g_final.reshape(1, d))


def kernel(x, ln_mix, w_in, sb_out_norm, swa_out_norm, swa_sinks, w_out, ln_ffn, w_gate_up, w_down, ln_final):
    batch, seq, d = x.shape
    depth = w_in.shape[0]
    xt = x.reshape(batch * seq, d)
    for l in range(depth):
        proj = _norm_proj(xt, ln_mix[l], w_in[l].astype(BF16), batch, seq)
        sb_o = _sb_attention(proj, batch, seq).reshape(batch * seq, SB_W)
        sw_o = _swa_attention(proj, swa_sinks[l], batch, seq).reshape(batch * seq, SWA_Q_W)
        xt = _out_proj(sb_o, sw_o, sb_out_norm[l], swa_out_norm[l], w_out[l].astype(BF16), xt)
        xt = _ffn(xt, ln_ffn[l], w_gate_up[l].astype(BF16), w_down[l].astype(BF16),
                  ln_final, final_norm=(l == depth - 1))
    return xt.reshape(batch, seq, d)
```

```python
import functools

import jax
import jax.numpy as jnp
from jax import lax
from jax.experimental import pallas as pl
from jax.experimental.pallas import tpu as pltpu

D_MODEL = 2048
HEAD_DIM = 64
SB_HEADS = 16
SWA_HEADS = 16
SWA_KV_HEADS = 4
SWA_GROUP = SWA_HEADS // SWA_KV_HEADS
WINDOW = 128
D_FF = 5632
RMS_EPS = 1e-5

SB_W = SB_HEADS * HEAD_DIM
SWA_Q_W = SWA_HEADS * HEAD_DIM
SWA_KV_W = SWA_KV_HEADS * HEAD_DIM
IN_COLS = 3 * SB_W + SWA_Q_W + 2 * SWA_KV_W
Q_SCALE = HEAD_DIM ** -0.5
LOG2E = 1.4426950408889634

LANES = 128
HEAD_PAIR = LANES // HEAD_DIM
VMEM_LIMIT = 56 * 1024 * 1024

F32 = jnp.float32
BF16 = jnp.bfloat16

PROJ_TM, PROJ_TN = 1024, 768
SB_T = 256
SB_SUB = 128
SWA_T = WINDOW
OUT_TM = 512
FFN_TM, FFN_TF = 512, 512


def _params(n_axes):
    return pltpu.CompilerParams(
        dimension_semantics=("arbitrary",) * n_axes,
        vmem_limit_bytes=VMEM_LIMIT)


def _rms(x, g):
    ms = jnp.mean(x * x, axis=-1, keepdims=True)
    return x * lax.rsqrt(ms + RMS_EPS) * g


def _dot(a, b):
    return jnp.dot(a, b, preferred_element_type=F32)


def _dot_nt(a, b):
    return lax.dot_general(a, b, (((1,), (1,)), ((), ())),
                           preferred_element_type=F32)


def _norm_proj_kernel(x_ref, g_ref, w_ref, o_ref, h_ref):
    @pl.when(pl.program_id(1) == 0)
    def _():
        h_ref[...] = _rms(x_ref[...], g_ref[...]).astype(BF16)

    res = _dot(h_ref[...], w_ref[...])
    for c in range(PROJ_TN // LANES):
        o_ref[0, c] = res[:, c * LANES:(c + 1) * LANES].astype(o_ref.dtype)


def _norm_proj(x, g, w, layer, batch, seq):
    t, d = x.shape
    n = w.shape[2]
    tiles_per_seq = seq // PROJ_TM
    blk_per_tile = PROJ_TN // LANES
    return pl.pallas_call(
        _norm_proj_kernel,
        grid=(t // PROJ_TM, n // PROJ_TN),
        in_specs=[
            pl.BlockSpec((PROJ_TM, d), lambda i, j: (i, 0)),
            pl.BlockSpec((1, d), lambda i, j: (0, 0)),
            pl.BlockSpec((None, d, PROJ_TN), lambda i, j: (layer, 0, j)),
        ],
        out_specs=pl.BlockSpec((1, blk_per_tile, PROJ_TM, LANES),
                               lambda i, j: (i // tiles_per_seq, j, i % tiles_per_seq, 0)),
        out_shape=jax.ShapeDtypeStruct((batch, n // LANES, seq, LANES), BF16),
        scratch_shapes=[pltpu.VMEM((PROJ_TM, d), BF16)],
        compiler_params=_params(2),
        name="norm_in_proj",
    )(x, g.reshape(1, d), w)


def _sb_kernel(q_ref, k_ref, v_ref, o_ref, vt_ref, z_ref, w_ref, acc_ref):
    i = pl.program_id(2)
    n_kblk = vt_ref.shape[0]

    @pl.when(i == 0)
    def _():
        def tr(c, _):
            r0 = pl.multiple_of(c * SB_T, SB_T)
            vt_ref[c] = v_ref[0, 0, pl.ds(r0, SB_T), :].astype(F32).T.astype(BF16)
            return 0
        lax.fori_loop(0, n_kblk, tr, 0)

    q_t = (q_ref[0, 0].astype(F32) * Q_SCALE).T
    feat = lax.broadcasted_iota(jnp.int32, q_t.shape, 0)
    qt = [jnp.where((feat // HEAD_DIM) == h, q_t, 0.0).astype(BF16) for h in range(HEAD_PAIR)]

    r = lax.broadcasted_iota(jnp.int32, (SB_SUB, 2 * SB_SUB), 0)
    j2 = lax.broadcasted_iota(jnp.int32, (SB_SUB, 2 * SB_SUB), 1)
    tri = jnp.where((j2 % SB_SUB) >= r, 1.0, 0.0).astype(BF16)

    srow = lax.broadcasted_iota(jnp.int32, (SB_T, SB_T), 0)
    tcol = lax.broadcasted_iota(jnp.int32, (SB_T, SB_T), 1)
    causal = srow < tcol

    n_sub = SB_T // SB_SUB
    heads = range(HEAD_PAIR)

    def scores(k, slot, h):
        j = jnp.maximum(i - k, 0)
        k0 = pl.multiple_of(j * SB_T, SB_T)
        z_ref[slot, h] = _dot(k_ref[0, 0, pl.ds(k0, SB_T), :], qt[h])

    def values(k, h):
        acc_ref[h] += _dot(vt_ref[i - k], w_ref[h])

    def weights(slot, h, c, diag):
        zt = z_ref[slot, h]
        sp = jnp.maximum(zt, 0.0) + jnp.log(1.0 + jnp.exp2(jnp.abs(zt) * -LOG2E))
        if diag:
            sp = jnp.where(causal, sp, 0.0)
        hi = sp.astype(BF16)
        lo = (sp - hi.astype(F32)).astype(BF16)
        xs = [None] * n_sub
        for sb in reversed(range(n_sub)):
            r0, r1 = sb * SB_SUB, (sb + 1) * SB_SUB
            cs = _dot(tri, jnp.concatenate([hi[r0:r1], lo[r0:r1]], axis=0))
            xs[sb] = (zt[r0:r1] - cs) - c
            c = c + cs[0:1, :]
        w = jnp.exp(jnp.concatenate(xs, axis=0))
        if diag:
            w = jnp.where(causal, w, 0.0)
        w_ref[h] = w.astype(BF16)
        return c

    acc_ref[...] = jnp.zeros_like(acc_ref)
    for h in heads:
        scores(0, 0, h)
    for h in heads:
        scores(1, 1, h)
    cs = tuple(weights(0, h, jnp.zeros((1, SB_T), F32), True) for h in heads)
    for h in heads:
        scores(2, 0, h)

    def pair(m, cs):
        k = 2 * m + 1
        for h in heads:
            values(k - 1, h)
        cs = tuple(weights(1, h, cs[h], False) for h in heads)
        for h in heads:
            scores(k + 2, 1, h)
        for h in heads:
            values(k, h)
        cs = tuple(weights(0, h, cs[h], False) for h in heads)
        for h in heads:
            scores(k + 3, 0, h)
        return cs

    cs = lax.fori_loop(0, lax.shift_right_logical(i, 1), pair, cs)

    @pl.when((i & 1) == 1)
    def _():
        for h in heads:
            values(i - 1, h)
        for h in heads:
            weights(1, h, cs[h], False)

    for h in heads:
        values(i, h)

    row = lax.broadcasted_iota(jnp.int32, (LANES, SB_T), 0)
    acc_t = jnp.where(row < HEAD_DIM, acc_ref[0], acc_ref[1])
    o_ref[0] = acc_t.T


def _sb_attention(proj, batch, seq):
    n_pairs = SB_HEADS // HEAD_PAIR
    k_off = SB_W // LANES
    v_off = 2 * SB_W // LANES
    return pl.pallas_call(
        _sb_kernel,
        grid=(batch, n_pairs, seq // SB_T),
        in_specs=[
            pl.BlockSpec((1, 1, SB_T, LANES), lambda b, p, i: (b, p, i, 0)),
            pl.BlockSpec((1, 1, seq, LANES), lambda b, p, i: (b, k_off + p, 0, 0)),
            pl.BlockSpec((1, 1, seq, LANES), lambda b, p, i: (b, v_off + p, 0, 0)),
        ],
        out_specs=pl.BlockSpec((1, SB_T, LANES), lambda b, p, i: (b, i, p)),
        out_shape=jax.ShapeDtypeStruct((batch, seq, SB_W), F32),
        scratch_shapes=[
            pltpu.VMEM((seq // SB_T, LANES, SB_T), BF16),
            pltpu.VMEM((2, HEAD_PAIR, SB_T, SB_T), F32),
            pltpu.VMEM((HEAD_PAIR, SB_T, SB_T), BF16),
            pltpu.VMEM((HEAD_PAIR, LANES, SB_T), F32),
        ],
        compiler_params=_params(3),
        name="stick_breaking_attention",
    )(proj, proj, proj)


def _swa_kernel(sink_ref, q_ref, k_ref, v_ref, o_ref, vt_ref):
    i = pl.program_id(1)
    n_blk = vt_ref.shape[0]

    @pl.when(i == 0)
    def _():
        def tr(c, _):
            r0 = pl.multiple_of(c * SWA_T, SWA_T)
            for m in range(SWA_KV_W // LANES):
                vt_ref[c, m * LANES:(m + 1) * LANES, :] = (
                    v_ref[0, m, pl.ds(r0, SWA_T), :].astype(F32).T.astype(BF16))
            return 0
        lax.fori_loop(0, n_blk, tr, 0)

    prev = jnp.maximum(i - 1, 0)
    p0 = pl.multiple_of(prev * SWA_T, SWA_T)
    c0 = pl.multiple_of(i * SWA_T, SWA_T)
    kband = [jnp.concatenate([k_ref[0, m, pl.ds(p0, SWA_T), :],
                              k_ref[0, m, pl.ds(c0, SWA_T), :]], axis=0)
             for m in range(SWA_KV_W // LANES)]
    vtband = jnp.concatenate([vt_ref[prev], vt_ref[i]], axis=1)

    band = 2 * SWA_T
    c = lax.broadcasted_iota(jnp.int32, (band, SWA_T), 0)
    r = lax.broadcasted_iota(jnp.int32, (band, SWA_T), 1)
    dist = r + SWA_T - c
    valid = (dist >= 0) & (dist < WINDOW) & ((c >= SWA_T) | (i > 0))
    dist_f = dist.astype(F32)
    neg_inf = jnp.float32(-jnp.inf)

    o_heads = []
    for g in range(SWA_KV_HEADS):
        gs = slice(g * HEAD_DIM, (g + 1) * HEAD_DIM)
        group = range(g * SWA_GROUP, (g + 1) * SWA_GROUP)
        half = lambda n: slice((n % HEAD_PAIR) * HEAD_DIM, (n % HEAD_PAIR + 1) * HEAD_DIM)
        qg = jnp.concatenate(
            [q_ref[0, h // HEAD_PAIR, :, half(h)] for h in group], axis=0) * Q_SCALE
        zt = _dot_nt(kband[g // HEAD_PAIR][:, half(g)], qg)
        ps, dens = [], []
        for n, h in enumerate(group):
            slope = 2.0 ** (-8.0 * (h + 1.0) / SWA_HEADS)
            sink = sink_ref[h]
            s = jnp.where(valid, zt[:, n * SWA_T:(n + 1) * SWA_T] - slope * dist_f, neg_inf)
            mx = jnp.maximum(jnp.max(s, axis=0, keepdims=True), sink)
            p = jnp.exp(s - mx)
            dens.append(jnp.sum(p, axis=0, keepdims=True) + jnp.exp(sink - mx))
            ps.append(p.astype(BF16))
        o_t = _dot(vtband[gs, :], jnp.concatenate(ps, axis=1))
        for n in range(SWA_GROUP):
            o_heads.append(o_t[:, n * SWA_T:(n + 1) * SWA_T] / dens[n])

    for m in range(SWA_HEADS // HEAD_PAIR):
        pair = jnp.concatenate(o_heads[HEAD_PAIR * m:HEAD_PAIR * (m + 1)], axis=0)
        o_ref[0, :, m * LANES:(m + 1) * LANES] = pair.T


def _swa_attention(proj, sinks, batch, seq):
    q_n, kv_n = SWA_Q_W // LANES, SWA_KV_W // LANES
    q_blk = 3 * SB_W // SWA_Q_W
    k_blk = (3 * SB_W + SWA_Q_W) // SWA_KV_W
    v_blk = k_blk + 1
    return pl.pallas_call(
        _swa_kernel,
        grid=(batch, seq // SWA_T),
        in_specs=[
            pl.BlockSpec(memory_space=pltpu.SMEM),
            pl.BlockSpec((1, q_n, SWA_T, LANES), lambda b, i: (b, q_blk, i, 0)),
            pl.BlockSpec((1, kv_n, seq, LANES), lambda b, i: (b, k_blk, 0, 0)),
            pl.BlockSpec((1, kv_n, seq, LANES), lambda b, i: (b, v_blk, 0, 0)),
        ],
        out_specs=pl.BlockSpec((1, SWA_T, SWA_Q_W), lambda b, i: (b, i, 0)),
        out_shape=jax.ShapeDtypeStruct((batch, seq, SWA_Q_W), F32),
        scratch_shapes=[pltpu.VMEM((seq // SWA_T, SWA_KV_W, SWA_T), BF16)],
        compiler_params=_params(2),
        name="sliding_window_attention",
    )(sinks, proj, proj, proj)


def _out_proj_kernel(sb_ref, sw_ref, ga_ref, gb_ref, w_ref, x_ref, o_ref):
    mixed = jnp.concatenate([_rms(sb_ref[...], ga_ref[...]).astype(BF16),
                             _rms(sw_ref[...], gb_ref[...]).astype(BF16)], axis=1)
    o_ref[...] = x_ref[...] + _dot(mixed, w_ref[...])


def _out_proj(sb_o, sw_o, ga, gb, w, layer, x):
    t, d = x.shape
    mix = SB_W + SWA_Q_W
    return pl.pallas_call(
        _out_proj_kernel,
        grid=(t // OUT_TM,),
        in_specs=[
            pl.BlockSpec((OUT_TM, SB_W), lambda i: (i, 0)),
            pl.BlockSpec((OUT_TM, SWA_Q_W), lambda i: (i, 0)),
            pl.BlockSpec((1, SB_W), lambda i: (0, 0)),
            pl.BlockSpec((1, SWA_Q_W), lambda i: (0, 0)),
            pl.BlockSpec((None, mix, d), lambda i: (layer, 0, 0)),
            pl.BlockSpec((OUT_TM, d), lambda i: (i, 0)),
        ],
        out_specs=pl.BlockSpec((OUT_TM, d), lambda i: (i, 0)),
        out_shape=jax.ShapeDtypeStruct((t, d), F32),
        compiler_params=_params(1),
        name="out_proj",
    )(sb_o, sw_o, ga.reshape(1, SB_W), gb.reshape(1, SWA_Q_W), w, x)


def _ffn_kernel(x_ref, g_ref, wg_ref, wu_ref, wd_ref, gf_ref, o_ref, h_ref, acc_ref, *, final_norm):
    f = pl.program_id(1)

    @pl.when(f == 0)
    def _():
        h_ref[...] = _rms(x_ref[...], g_ref[...]).astype(BF16)
        acc_ref[...] = jnp.zeros_like(acc_ref)

    h = h_ref[...]
    gate = _dot(h, wg_ref[...])
    up = _dot(h, wu_ref[...])
    act = (gate * jax.nn.sigmoid(gate) * up).astype(BF16)
    acc_ref[...] += _dot(act, wd_ref[...])

    @pl.when(f == pl.num_programs(1) - 1)
    def _():
        y = x_ref[...] + acc_ref[...]
        if final_norm:
            y = _rms(y, gf_ref[...])
        o_ref[...] = y


def _ffn(x, g, w_gate_up, w_down, layer, g_final, final_norm):
    t, d = x.shape
    n_f = D_FF // FFN_TF
    return pl.pallas_call(
        functools.partial(_ffn_kernel, final_norm=final_norm),
        grid=(t // FFN_TM, n_f),
        in_specs=[
            pl.BlockSpec((FFN_TM, d), lambda i, f: (i, 0)),
            pl.BlockSpec((1, d), lambda i, f: (0, 0)),
            pl.BlockSpec((None, d, FFN_TF), lambda i, f: (layer, 0, f)),
            pl.BlockSpec((None, d, FFN_TF), lambda i, f: (layer, 0, f + n_f)),
            pl.BlockSpec((None, FFN_TF, d), lambda i, f: (layer, f, 0)),
            pl.BlockSpec((1, d), lambda i, f: (0, 0)),
        ],
        out_specs=pl.BlockSpec((FFN_TM, d), lambda i, f: (i, 0)),
        out_shape=jax.ShapeDtypeStruct((t, d), F32),
        scratch_shapes=[pltpu.VMEM((FFN_TM, d), BF16), pltpu.VMEM((FFN_TM, d), F32)],
        compiler_params=_params(2),
        name="swiglu_ffn",
    )(x, g.reshape(1, d), w_gate_up, w_gate_up, w_down, g_final.reshape(1, d))


def kernel(x, ln_mix, w_in, sb_out_norm, swa_out_norm, swa_sinks, w_out, ln_ffn, w_gate_up, w_down, ln_final):
    batch, seq, d = x.shape
    depth = w_in.shape[0]
    xt = x.reshape(batch * seq, d)
    w_in, w_out, w_gate_up, w_down = (w.astype(BF16) for w in (w_in, w_out, w_gate_up, w_down))
    for l in range(depth):
        proj = _norm_proj(xt, ln_mix[l], w_in, l, batch, seq)
        sb_o = _sb_attention(proj, batch, seq).reshape(batch * seq, SB_W)
        sw_o = _swa_attention(proj, swa_sinks[l], batch, seq).reshape(batch * seq, SWA_Q_W)
        xt = _out_proj(sb_o, sw_o, sb_out_norm[l], swa_out_norm[l], w_out, l, xt)
        xt = _ffn(xt, ln_ffn[l], w_gate_up, w_down, l, ln_final, final_norm=(l == depth - 1))
    return xt.reshape(batch, seq, d)
```

```python
import functools

import jax
import jax.numpy as jnp
from jax import lax
from jax.experimental import pallas as pl
from jax.experimental.pallas import tpu as pltpu

D_MODEL = 2048
HEAD_DIM = 64
SB_HEADS = 16
SWA_HEADS = 16
SWA_KV_HEADS = 4
SWA_GROUP = SWA_HEADS // SWA_KV_HEADS
WINDOW = 128
D_FF = 5632
RMS_EPS = 1e-5

SB_W = SB_HEADS * HEAD_DIM
SWA_Q_W = SWA_HEADS * HEAD_DIM
SWA_KV_W = SWA_KV_HEADS * HEAD_DIM
IN_COLS = 3 * SB_W + SWA_Q_W + 2 * SWA_KV_W
Q_SCALE = HEAD_DIM ** -0.5
LOG2E = 1.4426950408889634

LANES = 128
HEAD_PAIR = LANES // HEAD_DIM
VMEM_LIMIT = 56 * 1024 * 1024

F32 = jnp.float32
BF16 = jnp.bfloat16

PROJ_TM, PROJ_TN = 1024, 768
SB_T = 256
SB_SUB = 128
SWA_T = WINDOW
OUT_TM = 512
FFN_TM, FFN_TF = 512, 512


def _params(n_axes):
    return pltpu.CompilerParams(
        dimension_semantics=("arbitrary",) * n_axes,
        vmem_limit_bytes=VMEM_LIMIT)


def _rms(x, g):
    ms = jnp.mean(x * x, axis=-1, keepdims=True)
    return x * lax.rsqrt(ms + RMS_EPS) * g


def _dot(a, b):
    return jnp.dot(a, b, preferred_element_type=F32)


def _dot_nt(a, b):
    return lax.dot_general(a, b, (((1,), (1,)), ((), ())),
                           preferred_element_type=F32)


def _norm_proj_kernel(x_ref, g_ref, w_ref, o_ref, h_ref):
    @pl.when(pl.program_id(1) == 0)
    def _():
        h_ref[...] = _rms(x_ref[...], g_ref[...]).astype(BF16)

    res = _dot(h_ref[...], w_ref[...])
    for c in range(PROJ_TN // LANES):
        o_ref[0, c] = res[:, c * LANES:(c + 1) * LANES].astype(o_ref.dtype)


def _norm_proj(x, g, w, layer, batch, seq):
    t, d = x.shape
    n = w.shape[2]
    tiles_per_seq = seq // PROJ_TM
    blk_per_tile = PROJ_TN // LANES
    return pl.pallas_call(
        _norm_proj_kernel,
        grid=(t // PROJ_TM, n // PROJ_TN),
        in_specs=[
            pl.BlockSpec((PROJ_TM, d), lambda i, j: (i, 0)),
            pl.BlockSpec((1, d), lambda i, j: (0, 0)),
            pl.BlockSpec((None, d, PROJ_TN), lambda i, j: (layer, 0, j)),
        ],
        out_specs=pl.BlockSpec((1, blk_per_tile, PROJ_TM, LANES),
                               lambda i, j: (i // tiles_per_seq, j, i % tiles_per_seq, 0)),
        out_shape=jax.ShapeDtypeStruct((batch, n // LANES, seq, LANES), BF16),
        scratch_shapes=[pltpu.VMEM((PROJ_TM, d), BF16)],
        compiler_params=_params(2),
        name="norm_in_proj",
    )(x, g.reshape(1, d), w)


def _sb_kernel(q_ref, k_ref, v_ref, o_ref, vt_ref, z_ref, w_ref, acc_ref):
    i = pl.program_id(2)
    n_kblk = vt_ref.shape[0]

    @pl.when(i == 0)
    def _():
        def tr(c, _):
            r0 = pl.multiple_of(c * SB_T, SB_T)
            vt_ref[c] = v_ref[0, 0, pl.ds(r0, SB_T), :].astype(F32).T.astype(BF16)
            return 0
        lax.fori_loop(0, n_kblk, tr, 0)

    q_t = (q_ref[0, 0].astype(F32) * Q_SCALE).T
    feat = lax.broadcasted_iota(jnp.int32, q_t.shape, 0)
    qt = [jnp.where((feat // HEAD_DIM) == h, q_t, 0.0).astype(BF16) for h in range(HEAD_PAIR)]

    r = lax.broadcasted_iota(jnp.int32, (SB_SUB, SB_SUB), 0)
    j2 = lax.broadcasted_iota(jnp.int32, (SB_SUB, SB_SUB), 1)
    tri = jnp.where(j2 >= r, 1.0, 0.0).astype(BF16)

    srow = lax.broadcasted_iota(jnp.int32, (SB_T, SB_T), 0)
    tcol = lax.broadcasted_iota(jnp.int32, (SB_T, SB_T), 1)
    causal = srow < tcol

    n_sub = SB_T // SB_SUB
    heads = range(HEAD_PAIR)

    def scores(k, slot, h):
        j = jnp.maximum(i - k, 0)
        k0 = pl.multiple_of(j * SB_T, SB_T)
        z_ref[slot, h] = _dot(k_ref[0, 0, pl.ds(k0, SB_T), :], qt[h])

    def values(k, h):
        acc_ref[h] += _dot(vt_ref[i - k], w_ref[h])

    def weights(slot, h, c, diag):
        zt = z_ref[slot, h]
        sp = jnp.maximum(zt, 0.0) + jnp.log(1.0 + jnp.exp2(jnp.abs(zt) * -LOG2E))
        if diag:
            sp = jnp.where(causal, sp, 0.0)
        spb = sp.astype(BF16)
        xs = [None] * n_sub
        for sb in reversed(range(n_sub)):
            r0, r1 = sb * SB_SUB, (sb + 1) * SB_SUB
            cs = _dot(tri, spb[r0:r1])
            xs[sb] = (zt[r0:r1] - cs) - c
            c = c + cs[0:1, :]
        w = jnp.exp(jnp.concatenate(xs, axis=0))
        if diag:
            w = jnp.where(causal, w, 0.0)
        w_ref[h] = w.astype(BF16)
        return c

    acc_ref[...] = jnp.zeros_like(acc_ref)
    for h in heads:
        scores(0, 0, h)
    for h in heads:
        scores(1, 1, h)
    cs = tuple(weights(0, h, jnp.zeros((1, SB_T), F32), True) for h in heads)
    for h in heads:
        scores(2, 0, h)

    def pair(m, cs):
        k = 2 * m + 1
        for h in heads:
            values(k - 1, h)
        out = []
        for h in heads:
            out.append(weights(1, h, cs[h], False))
            scores(k + 2, 1, h)
        cs = tuple(out)
        for h in heads:
            values(k, h)
        out = []
        for h in heads:
            out.append(weights(0, h, cs[h], False))
            scores(k + 3, 0, h)
        return tuple(out)

    cs = lax.fori_loop(0, lax.shift_right_logical(i, 1), pair, cs)

    @pl.when((i & 1) == 1)
    def _():
        for h in heads:
            values(i - 1, h)
        for h in heads:
            weights(1, h, cs[h], False)

    for h in heads:
        values(i, h)

    row = lax.broadcasted_iota(jnp.int32, (LANES, SB_T), 0)
    acc_t = jnp.where(row < HEAD_DIM, acc_ref[0], acc_ref[1])
    o_ref[0] = acc_t.T


def _sb_attention(proj, batch, seq):
    n_pairs = SB_HEADS // HEAD_PAIR
    k_off = SB_W // LANES
    v_off = 2 * SB_W // LANES
    return pl.pallas_call(
        _sb_kernel,
        grid=(batch, n_pairs, seq // SB_T),
        in_specs=[
            pl.BlockSpec((1, 1, SB_T, LANES), lambda b, p, i: (b, p, i, 0)),
            pl.BlockSpec((1, 1, seq, LANES), lambda b, p, i: (b, k_off + p, 0, 0)),
            pl.BlockSpec((1, 1, seq, LANES), lambda b, p, i: (b, v_off + p, 0, 0)),
        ],
        out_specs=pl.BlockSpec((1, SB_T, LANES), lambda b, p, i: (b, i, p)),
        out_shape=jax.ShapeDtypeStruct((batch, seq, SB_W), F32),
        scratch_shapes=[
            pltpu.VMEM((seq // SB_T, LANES, SB_T), BF16),
            pltpu.VMEM((2, HEAD_PAIR, SB_T, SB_T), F32),
            pltpu.VMEM((HEAD_PAIR, SB_T, SB_T), BF16),
            pltpu.VMEM((HEAD_PAIR, LANES, SB_T), F32),
        ],
        compiler_params=_params(3),
        name="stick_breaking_attention",
    )(proj, proj, proj)


def _swa_kernel(sink_ref, q_ref, k_ref, v_ref, o_ref, vt_ref):
    i = pl.program_id(1)
    n_blk = vt_ref.shape[0]

    @pl.when(i == 0)
    def _():
        def tr(c, _):
            r0 = pl.multiple_of(c * SWA_T, SWA_T)
            for m in range(SWA_KV_W // LANES):
                vt_ref[c, m * LANES:(m + 1) * LANES, :] = (
                    v_ref[0, m, pl.ds(r0, SWA_T), :].astype(F32).T.astype(BF16))
            return 0
        lax.fori_loop(0, n_blk, tr, 0)

    prev = jnp.maximum(i - 1, 0)
    p0 = pl.multiple_of(prev * SWA_T, SWA_T)
    c0 = pl.multiple_of(i * SWA_T, SWA_T)
    kband = [jnp.concatenate([k_ref[0, m, pl.ds(p0, SWA_T), :],
                              k_ref[0, m, pl.ds(c0, SWA_T), :]], axis=0)
             for m in range(SWA_KV_W // LANES)]
    vtband = jnp.concatenate([vt_ref[prev], vt_ref[i]], axis=1)

    band = 2 * SWA_T
    c = lax.broadcasted_iota(jnp.int32, (band, SWA_T), 0)
    r = lax.broadcasted_iota(jnp.int32, (band, SWA_T), 1)
    dist = r + SWA_T - c
    valid = (dist >= 0) & (dist < WINDOW) & ((c >= SWA_T) | (i > 0))
    dist_f = dist.astype(F32)
    neg_inf = jnp.float32(-jnp.inf)

    o_heads = []
    for g in range(SWA_KV_HEADS):
        gs = slice(g * HEAD_DIM, (g + 1) * HEAD_DIM)
        group = range(g * SWA_GROUP, (g + 1) * SWA_GROUP)
        half = lambda n: slice((n % HEAD_PAIR) * HEAD_DIM, (n % HEAD_PAIR + 1) * HEAD_DIM)
        qg = jnp.concatenate(
            [q_ref[0, h // HEAD_PAIR, :, half(h)] for h in group], axis=0) * Q_SCALE
        zt = _dot_nt(kband[g // HEAD_PAIR][:, half(g)], qg)
        ps, dens = [], []
        for n, h in enumerate(group):
            slope = 2.0 ** (-8.0 * (h + 1.0) / SWA_HEADS)
            sink = sink_ref[h]
            s = jnp.where(valid, zt[:, n * SWA_T:(n + 1) * SWA_T] - slope * dist_f, neg_inf)
            mx = jnp.maximum(jnp.max(s, axis=0, keepdims=True), sink)
            p = jnp.exp(s - mx)
            dens.append(jnp.sum(p, axis=0, keepdims=True) + jnp.exp(sink - mx))
            ps.append(p.astype(BF16))
        o_t = _dot(vtband[gs, :], jnp.concatenate(ps, axis=1))
        for n in range(SWA_GROUP):
            o_heads.append(o_t[:, n * SWA_T:(n + 1) * SWA_T] / dens[n])

    for m in range(SWA_HEADS // HEAD_PAIR):
        pair = jnp.concatenate(o_heads[HEAD_PAIR * m:HEAD_PAIR * (m + 1)], axis=0)
        o_ref[0, :, m * LANES:(m + 1) * LANES] = pair.T


def _swa_attention(proj, sinks, batch, seq):
    q_n, kv_n = SWA_Q_W // LANES, SWA_KV_W // LANES
    q_blk = 3 * SB_W // SWA_Q_W
    k_blk = (3 * SB_W + SWA_Q_W) // SWA_KV_W
    v_blk = k_blk + 1
    return pl.pallas_call(
        _swa_kernel,
        grid=(batch, seq // SWA_T),
        in_specs=[
            pl.BlockSpec(memory_space=pltpu.SMEM),
            pl.BlockSpec((1, q_n, SWA_T, LANES), lambda b, i: (b, q_blk, i, 0)),
            pl.BlockSpec((1, kv_n, seq, LANES), lambda b, i: (b, k_blk, 0, 0)),
            pl.BlockSpec((1, kv_n, seq, LANES), lambda b, i: (b, v_blk, 0, 0)),
        ],
        out_specs=pl.BlockSpec((1, SWA_T, SWA_Q_W), lambda b, i: (b, i, 0)),
        out_shape=jax.ShapeDtypeStruct((batch, seq, SWA_Q_W), F32),
        scratch_shapes=[pltpu.VMEM((seq // SWA_T, SWA_KV_W, SWA_T), BF16)],
        compiler_params=_params(2),
        name="sliding_window_attention",
    )(sinks, proj, proj, proj)


def _out_proj_kernel(sb_ref, sw_ref, ga_ref, gb_ref, w_ref, x_ref, o_ref):
    mixed = jnp.concatenate([_rms(sb_ref[...], ga_ref[...]).astype(BF16),
                             _rms(sw_ref[...], gb_ref[...]).astype(BF16)], axis=1)
    o_ref[...] = x_ref[...] + _dot(mixed, w_ref[...])


def _out_proj(sb_o, sw_o, ga, gb, w, layer, x):
    t, d = x.shape
    mix = SB_W + SWA_Q_W
    return pl.pallas_call(
        _out_proj_kernel,
        grid=(t // OUT_TM,),
        in_specs=[
            pl.BlockSpec((OUT_TM, SB_W), lambda i: (i, 0)),
            pl.BlockSpec((OUT_TM, SWA_Q_W), lambda i: (i, 0)),
            pl.BlockSpec((1, SB_W), lambda i: (0, 0)),
            pl.BlockSpec((1, SWA_Q_W), lambda i: (0, 0)),
            pl.BlockSpec((None, mix, d), lambda i: (layer, 0, 0)),
            pl.BlockSpec((OUT_TM, d), lambda i: (i, 0)),
        ],
        out_specs=pl.BlockSpec((OUT_TM, d), lambda i: (i, 0)),
        out_shape=jax.ShapeDtypeStruct((t, d), F32),
        compiler_params=_params(1),
        name="out_proj",
    )(sb_o, sw_o, ga.reshape(1, SB_W), gb.reshape(1, SWA_Q_W), w, x)


def _ffn_kernel(x_ref, g_ref, wg_ref, wu_ref, wd_ref, gf_ref, o_ref, h_ref, acc_ref, *, final_norm):
    f = pl.program_id(1)

    @pl.when(f == 0)
    def _():
        h_ref[...] = _rms(x_ref[...], g_ref[...]).astype(BF16)
        acc_ref[...] = jnp.zeros_like(acc_ref)

    h = h_ref[...]
    gate = _dot(h, wg_ref[...])
    up = _dot(h, wu_ref[...])
    act = (gate * jax.nn.sigmoid(gate) * up).astype(BF16)
    acc_ref[...] += _dot(act, wd_ref[...])

    @pl.when(f == pl.num_programs(1) - 1)
    def _():
        y = x_ref[...] + acc_ref[...]
        if final_norm:
            y = _rms(y, gf_ref[...])
        o_ref[...] = y


def _ffn(x, g, w_gate_up, w_down, layer, g_final, final_norm):
    t, d = x.shape
    n_f = D_FF // FFN_TF
    return pl.pallas_call(
        functools.partial(_ffn_kernel, final_norm=final_norm),
        grid=(t // FFN_TM, n_f),
        in_specs=[
            pl.BlockSpec((FFN_TM, d), lambda i, f: (i, 0)),
            pl.BlockSpec((1, d), lambda i, f: (0, 0)),
            pl.BlockSpec((None, d, FFN_TF), lambda i, f: (layer, 0, f)),
            pl.BlockSpec((None, d, FFN_TF), lambda i, f: (layer, 0, f + n_f)),
            pl.BlockSpec((None, FFN_TF, d), lambda i, f: (layer, f, 0)),
            pl.BlockSpec((1, d), lambda i, f: (0, 0)),
        ],
        out_specs=pl.BlockSpec((FFN_TM, d), lambda i, f: (i, 0)),
        out_shape=jax.ShapeDtypeStruct((t, d), F32),
        scratch_shapes=[pltpu.VMEM((FFN_TM, d), BF16), pltpu.VMEM((FFN_TM, d), F32)],
        compiler_params=_params(2),
        name="swiglu_ffn",
    )(x, g.reshape(1, d), w_gate_up, w_gate_up, w_down, g_final.reshape(1, d))


def kernel(x, ln_mix, w_in, sb_out_norm, swa_out_norm, swa_sinks, w_out, ln_ffn, w_gate_up, w_down, ln_final):
    batch, seq, d = x.shape
    depth = w_in.shape[0]
    xt = x.reshape(batch * seq, d)
    w_in, w_out, w_gate_up, w_down = (w.astype(BF16) for w in (w_in, w_out, w_gate_up, w_down))
    for l in range(depth):
        proj = _norm_proj(xt, ln_mix[l], w_in, l, batch, seq)
        sb_o = _sb_attention(proj, batch, seq).reshape(batch * seq, SB_W)
        sw_o = _swa_attention(proj, swa_sinks[l], batch, seq).reshape(batch * seq, SWA_Q_W)
        xt = _out_proj(sb_o, sw_o, sb_out_norm[l], swa_out_norm[l], w_out, l, xt)
        xt = _ffn(xt, ln_ffn[l], w_gate_up, w_down, l, ln_final, final_norm=(l == depth - 1))
    return xt.reshape(batch, seq, d)
```

```python
import functools

import jax
import jax.numpy as jnp
from jax import lax
from jax.experimental import pallas as pl
from jax.experimental.pallas import tpu as pltpu

D_MODEL = 2048
HEAD_DIM = 64
SB_HEADS = 16
SWA_HEADS = 16
SWA_KV_HEADS = 4
SWA_GROUP = SWA_HEADS // SWA_KV_HEADS
WINDOW = 128
D_FF = 5632
RMS_EPS = 1e-5

SB_W = SB_HEADS * HEAD_DIM
SWA_Q_W = SWA_HEADS * HEAD_DIM
SWA_KV_W = SWA_KV_HEADS * HEAD_DIM
IN_COLS = 3 * SB_W + SWA_Q_W + 2 * SWA_KV_W
Q_SCALE = HEAD_DIM ** -0.5
LOG2E = 1.4426950408889634

LANES = 128
HEAD_PAIR = LANES // HEAD_DIM
VMEM_LIMIT = 56 * 1024 * 1024

F32 = jnp.float32
BF16 = jnp.bfloat16

PROJ_TM, PROJ_TN = 1024, 768
SB_T = 256
SB_SUB = 128
SWA_T = WINDOW
OUT_TM = 512
FFN_TM, FFN_TF = 1024, 256


def _params(n_axes):
    return pltpu.CompilerParams(
        dimension_semantics=("arbitrary",) * n_axes,
        vmem_limit_bytes=VMEM_LIMIT)


def _rms(x, g):
    ms = jnp.mean(x * x, axis=-1, keepdims=True)
    return x * lax.rsqrt(ms + RMS_EPS) * g


def _dot(a, b):
    return jnp.dot(a, b, preferred_element_type=F32)


def _dot_nt(a, b):
    return lax.dot_general(a, b, (((1,), (1,)), ((), ())),
                           preferred_element_type=F32)


def _norm_proj_kernel(x_ref, g_ref, w_ref, o_ref, h_ref):
    @pl.when(pl.program_id(1) == 0)
    def _():
        h_ref[...] = _rms(x_ref[...], g_ref[...]).astype(BF16)

    res = _dot(h_ref[...], w_ref[...].astype(BF16))
    for c in range(PROJ_TN // LANES):
        o_ref[0, c] = res[:, c * LANES:(c + 1) * LANES].astype(o_ref.dtype)


def _norm_proj(x, g, w, layer, batch, seq):
    t, d = x.shape
    n = w.shape[2]
    tiles_per_seq = seq // PROJ_TM
    blk_per_tile = PROJ_TN // LANES
    return pl.pallas_call(
        _norm_proj_kernel,
        grid=(t // PROJ_TM, n // PROJ_TN),
        in_specs=[
            pl.BlockSpec((PROJ_TM, d), lambda i, j: (i, 0)),
            pl.BlockSpec((1, d), lambda i, j: (0, 0)),
            pl.BlockSpec((None, d, PROJ_TN), lambda i, j: (layer, 0, j)),
        ],
        out_specs=pl.BlockSpec((1, blk_per_tile, PROJ_TM, LANES),
                               lambda i, j: (i // tiles_per_seq, j, i % tiles_per_seq, 0)),
        out_shape=jax.ShapeDtypeStruct((batch, n // LANES, seq, LANES), BF16),
        scratch_shapes=[pltpu.VMEM((PROJ_TM, d), BF16)],
        compiler_params=_params(2),
        name="norm_in_proj",
    )(x, g.reshape(1, d), w)


def _sb_kernel(q_ref, k_ref, v_ref, o_ref, vt_ref, z_ref, w_ref, acc_ref):
    i = pl.program_id(2)
    n_kblk = vt_ref.shape[0]

    @pl.when(i == 0)
    def _():
        def tr(c, _):
            r0 = pl.multiple_of(c * SB_T, SB_T)
            vt_ref[c] = v_ref[0, 0, pl.ds(r0, SB_T), :].astype(F32).T.astype(BF16)
            return 0
        lax.fori_loop(0, n_kblk, tr, 0)

    q_t = (q_ref[0, 0].astype(F32) * Q_SCALE).T
    feat = lax.broadcasted_iota(jnp.int32, q_t.shape, 0)
    qt = [jnp.where((feat // HEAD_DIM) == h, q_t, 0.0).astype(BF16) for h in range(HEAD_PAIR)]

    r = lax.broadcasted_iota(jnp.int32, (SB_SUB, SB_SUB), 0)
    j2 = lax.broadcasted_iota(jnp.int32, (SB_SUB, SB_SUB), 1)
    tri = jnp.where(j2 >= r, 1.0, 0.0).astype(BF16)

    srow = lax.broadcasted_iota(jnp.int32, (SB_T, SB_T), 0)
    tcol = lax.broadcasted_iota(jnp.int32, (SB_T, SB_T), 1)
    causal = srow < tcol

    n_sub = SB_T // SB_SUB
    heads = range(HEAD_PAIR)

    def scores(k, slot, h):
        j = jnp.maximum(i - k, 0)
        k0 = pl.multiple_of(j * SB_T, SB_T)
        z_ref[slot, h] = _dot(k_ref[0, 0, pl.ds(k0, SB_T), :], qt[h])

    def values(k, h):
        acc_ref[h] += _dot(vt_ref[i - k], w_ref[h])

    def weights(slot, h, c, diag):
        zt = z_ref[slot, h]
        sp = jnp.maximum(zt, 0.0) + jnp.log(1.0 + jnp.exp2(jnp.abs(zt) * -LOG2E))
        if diag:
            sp = jnp.where(causal, sp, 0.0)
        spb = sp.astype(BF16)
        xs = [None] * n_sub
        for sb in reversed(range(n_sub)):
            r0, r1 = sb * SB_SUB, (sb + 1) * SB_SUB
            cs = _dot(tri, spb[r0:r1])
            xs[sb] = (zt[r0:r1] - cs) - c
            c = c + cs[0:1, :]
        w = jnp.exp(jnp.concatenate(xs, axis=0))
        if diag:
            w = jnp.where(causal, w, 0.0)
        w_ref[h] = w.astype(BF16)
        return c

    acc_ref[...] = jnp.zeros_like(acc_ref)
    for h in heads:
        scores(0, 0, h)
    for h in heads:
        scores(1, 1, h)
    cs = tuple(weights(0, h, jnp.zeros((1, SB_T), F32), True) for h in heads)
    for h in heads:
        scores(2, 0, h)

    def pair(m, cs):
        k = 2 * m + 1
        for h in heads:
            values(k - 1, h)
        out = []
        for h in heads:
            out.append(weights(1, h, cs[h], False))
            scores(k + 2, 1, h)
        cs = tuple(out)
        for h in heads:
            values(k, h)
        out = []
        for h in heads:
            out.append(weights(0, h, cs[h], False))
            scores(k + 3, 0, h)
        return tuple(out)

    cs = lax.fori_loop(0, lax.shift_right_logical(i, 1), pair, cs)

    @pl.when((i & 1) == 1)
    def _():
        for h in heads:
            values(i - 1, h)
        for h in heads:
            weights(1, h, cs[h], False)

    for h in heads:
        values(i, h)

    row = lax.broadcasted_iota(jnp.int32, (LANES, SB_T), 0)
    acc_t = jnp.where(row < HEAD_DIM, acc_ref[0], acc_ref[1])
    o_ref[0] = acc_t.T


def _sb_attention(proj, batch, seq):
    n_pairs = SB_HEADS // HEAD_PAIR
    k_off = SB_W // LANES
    v_off = 2 * SB_W // LANES
    return pl.pallas_call(
        _sb_kernel,
        grid=(batch, n_pairs, seq // SB_T),
        in_specs=[
            pl.BlockSpec((1, 1, SB_T, LANES), lambda b, p, i: (b, p, i, 0)),
            pl.BlockSpec((1, 1, seq, LANES), lambda b, p, i: (b, k_off + p, 0, 0)),
            pl.BlockSpec((1, 1, seq, LANES), lambda b, p, i: (b, v_off + p, 0, 0)),
        ],
        out_specs=pl.BlockSpec((1, SB_T, LANES), lambda b, p, i: (b, i, p)),
        out_shape=jax.ShapeDtypeStruct((batch, seq, SB_W), F32),
        scratch_shapes=[
            pltpu.VMEM((seq // SB_T, LANES, SB_T), BF16),
            pltpu.VMEM((2, HEAD_PAIR, SB_T, SB_T), F32),
            pltpu.VMEM((HEAD_PAIR, SB_T, SB_T), BF16),
            pltpu.VMEM((HEAD_PAIR, LANES, SB_T), F32),
        ],
        compiler_params=_params(3),
        name="stick_breaking_attention",
    )(proj, proj, proj)


def _swa_kernel(sink_ref, q_ref, k_ref, v_ref, o_ref, vt_ref):
    i = pl.program_id(1)
    n_blk = vt_ref.shape[0]

    @pl.when(i == 0)
    def _():
        def tr(c, _):
            r0 = pl.multiple_of(c * SWA_T, SWA_T)
            for m in range(SWA_KV_W // LANES):
                vt_ref[c, m * LANES:(m + 1) * LANES, :] = (
                    v_ref[0, m, pl.ds(r0, SWA_T), :].astype(F32).T.astype(BF16))
            return 0
        lax.fori_loop(0, n_blk, tr, 0)

    prev = jnp.maximum(i - 1, 0)
    p0 = pl.multiple_of(prev * SWA_T, SWA_T)
    c0 = pl.multiple_of(i * SWA_T, SWA_T)
    kband = [jnp.concatenate([k_ref[0, m, pl.ds(p0, SWA_T), :],
                              k_ref[0, m, pl.ds(c0, SWA_T), :]], axis=0)
             for m in range(SWA_KV_W // LANES)]
    vtband = jnp.concatenate([vt_ref[prev], vt_ref[i]], axis=1)

    band = 2 * SWA_T
    c = lax.broadcasted_iota(jnp.int32, (band, SWA_T), 0)
    r = lax.broadcasted_iota(jnp.int32, (band, SWA_T), 1)
    dist = r + SWA_T - c
    valid = (dist >= 0) & (dist < WINDOW) & ((c >= SWA_T) | (i > 0))
    dist_f = dist.astype(F32)
    neg_inf = jnp.float32(-jnp.inf)

    o_heads = []
    for g in range(SWA_KV_HEADS):
        gs = slice(g * HEAD_DIM, (g + 1) * HEAD_DIM)
        group = range(g * SWA_GROUP, (g + 1) * SWA_GROUP)
        half = lambda n: slice((n % HEAD_PAIR) * HEAD_DIM, (n % HEAD_PAIR + 1) * HEAD_DIM)
        qg = jnp.concatenate(
            [q_ref[0, h // HEAD_PAIR, :, half(h)] for h in group], axis=0) * Q_SCALE
        zt = _dot_nt(kband[g // HEAD_PAIR][:, half(g)], qg)
        ps, dens = [], []
        for n, h in enumerate(group):
            slope = 2.0 ** (-8.0 * (h + 1.0) / SWA_HEADS)
            sink = sink_ref[h]
            s = jnp.where(valid, zt[:, n * SWA_T:(n + 1) * SWA_T] - slope * dist_f, neg_inf)
            mx = jnp.maximum(jnp.max(s, axis=0, keepdims=True), sink)
            p = jnp.exp(s - mx)
            dens.append(jnp.sum(p, axis=0, keepdims=True) + jnp.exp(sink - mx))
            ps.append(p.astype(BF16))
        o_t = _dot(vtband[gs, :], jnp.concatenate(ps, axis=1))
        for n in range(SWA_GROUP):
            o_heads.append(o_t[:, n * SWA_T:(n + 1) * SWA_T] / dens[n])

    for m in range(SWA_HEADS // HEAD_PAIR):
        pair = jnp.concatenate(o_heads[HEAD_PAIR * m:HEAD_PAIR * (m + 1)], axis=0)
        o_ref[0, :, m * LANES:(m + 1) * LANES] = pair.T


def _swa_attention(proj, sinks, batch, seq):
    q_n, kv_n = SWA_Q_W // LANES, SWA_KV_W // LANES
    q_blk = 3 * SB_W // SWA_Q_W
    k_blk = (3 * SB_W + SWA_Q_W) // SWA_KV_W
    v_blk = k_blk + 1
    return pl.pallas_call(
        _swa_kernel,
        grid=(batch, seq // SWA_T),
        in_specs=[
            pl.BlockSpec(memory_space=pltpu.SMEM),
            pl.BlockSpec((1, q_n, SWA_T, LANES), lambda b, i: (b, q_blk, i, 0)),
            pl.BlockSpec((1, kv_n, seq, LANES), lambda b, i: (b, k_blk, 0, 0)),
            pl.BlockSpec((1, kv_n, seq, LANES), lambda b, i: (b, v_blk, 0, 0)),
        ],
        out_specs=pl.BlockSpec((1, SWA_T, SWA_Q_W), lambda b, i: (b, i, 0)),
        out_shape=jax.ShapeDtypeStruct((batch, seq, SWA_Q_W), F32),
        scratch_shapes=[pltpu.VMEM((seq // SWA_T, SWA_KV_W, SWA_T), BF16)],
        compiler_params=_params(2),
        name="sliding_window_attention",
    )(sinks, proj, proj, proj)


def _out_proj_kernel(sb_ref, sw_ref, ga_ref, gb_ref, w_ref, x_ref, o_ref):
    mixed = jnp.concatenate([_rms(sb_ref[...], ga_ref[...]).astype(BF16),
                             _rms(sw_ref[...], gb_ref[...]).astype(BF16)], axis=1)
    o_ref[...] = x_ref[...] + _dot(mixed, w_ref[...].astype(BF16))


def _out_proj(sb_o, sw_o, ga, gb, w, layer, x):
    t, d = x.shape
    mix = SB_W + SWA_Q_W
    return pl.pallas_call(
        _out_proj_kernel,
        grid=(t // OUT_TM,),
        in_specs=[
            pl.BlockSpec((OUT_TM, SB_W), lambda i: (i, 0)),
            pl.BlockSpec((OUT_TM, SWA_Q_W), lambda i: (i, 0)),
            pl.BlockSpec((1, SB_W), lambda i: (0, 0)),
            pl.BlockSpec((1, SWA_Q_W), lambda i: (0, 0)),
            pl.BlockSpec((None, mix, d), lambda i: (layer, 0, 0),
                         pipeline_mode=pl.Buffered(1)),
            pl.BlockSpec((OUT_TM, d), lambda i: (i, 0)),
        ],
        out_specs=pl.BlockSpec((OUT_TM, d), lambda i: (i, 0)),
        out_shape=jax.ShapeDtypeStruct((t, d), F32),
        compiler_params=_params(1),
        name="out_proj",
    )(sb_o, sw_o, ga.reshape(1, SB_W), gb.reshape(1, SWA_Q_W), w, x)


def _ffn_kernel(x_ref, g_ref, wg_ref, wu_ref, wd_ref, gf_ref, o_ref, h_ref, *, final_norm):
    f = pl.program_id(1)

    @pl.when(f == 0)
    def _():
        x = x_ref[...]
        h_ref[...] = _rms(x, g_ref[...]).astype(BF16)
        o_ref[...] = x

    h = h_ref[...]
    gate = _dot(h, wg_ref[...].astype(BF16))
    up = _dot(h, wu_ref[...].astype(BF16))
    act = (gate * jax.nn.sigmoid(gate) * up).astype(BF16)
    o_ref[...] += _dot(act, wd_ref[...].astype(BF16))

    if final_norm:
        @pl.when(f == pl.num_programs(1) - 1)
        def _():
            o_ref[...] = _rms(o_ref[...], gf_ref[...])


def _ffn(x, g, w_gate_up, w_down, layer, g_final, final_norm):
    t, d = x.shape
    n_f = D_FF // FFN_TF
    return pl.pallas_call(
        functools.partial(_ffn_kernel, final_norm=final_norm),
        grid=(t // FFN_TM, n_f),
        in_specs=[
            pl.BlockSpec((FFN_TM, d), lambda i, f: (i, 0)),
            pl.BlockSpec((1, d), lambda i, f: (0, 0)),
            pl.BlockSpec((None, d, FFN_TF), lambda i, f: (layer, 0, f)),
            pl.BlockSpec((None, d, FFN_TF), lambda i, f: (layer, 0, f + n_f)),
            pl.BlockSpec((None, FFN_TF, d), lambda i, f: (layer, f, 0)),
            pl.BlockSpec((1, d), lambda i, f: (0, 0)),
        ],
        out_specs=pl.BlockSpec((FFN_TM, d), lambda i, f: (i, 0)),
        out_shape=jax.ShapeDtypeStruct((t, d), F32),
        scratch_shapes=[pltpu.VMEM((FFN_TM, d), BF16)],
        compiler_params=_params(2),
        name="swiglu_ffn",
    )(x, g.reshape(1, d), w_gate_up, w_gate_up, w_down, g_final.reshape(1, d))


def kernel(x, ln_mix, w_in, sb_out_norm, swa_out_norm, swa_sinks, w_out, ln_ffn, w_gate_up, w_down, ln_final):
    batch, seq, d = x.shape
    depth = w_in.shape[0]
    xt = x.reshape(batch * seq, d)
    for l in range(depth):
        proj = _norm_proj(xt, ln_mix[l], w_in, l, batch, seq)
        sb_o = _sb_attention(proj, batch, seq).reshape(batch * seq, SB_W)
        sw_o = _swa_attention(proj, swa_sinks[l], batch, seq).reshape(batch * seq, SWA_Q_W)
        xt = _out_proj(sb_o, sw_o, sb_out_norm[l], swa_out_norm[l], w_out, l, xt)
        xt = _ffn(xt, ln_ffn[l], w_gate_up, w_down, l, ln_final, final_norm=(l == depth - 1))
    return xt.reshape(batch, seq, d)
```

```python
import functools

import jax
import jax.numpy as jnp
from jax import lax
from jax.experimental import pallas as pl
from jax.experimental.pallas import tpu as pltpu

D_MODEL = 2048
HEAD_DIM = 64
SB_HEADS = 16
SWA_HEADS = 16
SWA_KV_HEADS = 4
SWA_GROUP = SWA_HEADS // SWA_KV_HEADS
WINDOW = 128
D_FF = 5632
RMS_EPS = 1e-5

SB_W = SB_HEADS * HEAD_DIM
SWA_Q_W = SWA_HEADS * HEAD_DIM
SWA_KV_W = SWA_KV_HEADS * HEAD_DIM
IN_COLS = 3 * SB_W + SWA_Q_W + 2 * SWA_KV_W
Q_SCALE = HEAD_DIM ** -0.5
LOG2E = 1.4426950408889634

LANES = 128
HEAD_PAIR = LANES // HEAD_DIM
VMEM_LIMIT = 56 * 1024 * 1024

F32 = jnp.float32
BF16 = jnp.bfloat16

PROJ_TM, PROJ_TN = 1024, 768
SB_T = 256
SB_SUB = 128
N_SLOT = 3
SWA_T = WINDOW
OUT_TM = 512
FFN_TM, FFN_TF = 1024, 256


def _params(n_axes):
    return pltpu.CompilerParams(
        dimension_semantics=("arbitrary",) * n_axes,
        vmem_limit_bytes=VMEM_LIMIT)


def _rms(x, g):
    ms = jnp.mean(x * x, axis=-1, keepdims=True)
    return x * lax.rsqrt(ms + RMS_EPS) * g


def _dot(a, b):
    return jnp.dot(a, b, preferred_element_type=F32)


def _dot_nt(a, b):
    return lax.dot_general(a, b, (((1,), (1,)), ((), ())),
                           preferred_element_type=F32)


def _norm_proj_kernel(x_ref, g_ref, w_ref, o_ref, h_ref):
    @pl.when(pl.program_id(1) == 0)
    def _():
        h_ref[...] = _rms(x_ref[...], g_ref[...]).astype(BF16)

    res = _dot(h_ref[...], w_ref[...].astype(BF16))
    for c in range(PROJ_TN // LANES):
        o_ref[0, c] = res[:, c * LANES:(c + 1) * LANES].astype(o_ref.dtype)


def _norm_proj(x, g, w, layer, batch, seq):
    t, d = x.shape
    n = w.shape[2]
    tiles_per_seq = seq // PROJ_TM
    blk_per_tile = PROJ_TN // LANES
    return pl.pallas_call(
        _norm_proj_kernel,
        grid=(t // PROJ_TM, n // PROJ_TN),
        in_specs=[
            pl.BlockSpec((PROJ_TM, d), lambda i, j: (i, 0)),
            pl.BlockSpec((1, d), lambda i, j: (0, 0)),
            pl.BlockSpec((None, d, PROJ_TN), lambda i, j: (layer, 0, j)),
        ],
        out_specs=pl.BlockSpec((1, blk_per_tile, PROJ_TM, LANES),
                               lambda i, j: (i // tiles_per_seq, j, i % tiles_per_seq, 0)),
        out_shape=jax.ShapeDtypeStruct((batch, n // LANES, seq, LANES), BF16),
        scratch_shapes=[pltpu.VMEM((PROJ_TM, d), BF16)],
        compiler_params=_params(2),
        name="norm_in_proj",
    )(x, g.reshape(1, d), w)


def _sb_kernel(q_ref, k_ref, v_ref, o_ref, vt_ref, z_ref, w_ref, acc_ref):
    i = pl.program_id(2)
    n_kblk = vt_ref.shape[0]

    @pl.when(i == 0)
    def _():
        def tr(c, _):
            r0 = pl.multiple_of(c * SB_T, SB_T)
            vt_ref[c] = v_ref[0, 0, pl.ds(r0, SB_T), :].astype(F32).T.astype(BF16)
            return 0
        lax.fori_loop(0, n_kblk, tr, 0)

    q_t = (q_ref[0, 0].astype(F32) * Q_SCALE).T
    feat = lax.broadcasted_iota(jnp.int32, q_t.shape, 0)
    qt = [jnp.where((feat // HEAD_DIM) == h, q_t, 0.0).astype(BF16) for h in range(HEAD_PAIR)]

    r = lax.broadcasted_iota(jnp.int32, (SB_SUB, SB_SUB), 0)
    j2 = lax.broadcasted_iota(jnp.int32, (SB_SUB, SB_SUB), 1)
    tri = jnp.where(j2 >= r, 1.0, 0.0).astype(BF16)

    srow = lax.broadcasted_iota(jnp.int32, (SB_T, SB_T), 0)
    tcol = lax.broadcasted_iota(jnp.int32, (SB_T, SB_T), 1)
    causal = srow < tcol

    n_sub = SB_T // SB_SUB
    heads = range(HEAD_PAIR)

    def scores(k, slot, h):
        j = jnp.maximum(i - k, 0)
        k0 = pl.multiple_of(j * SB_T, SB_T)
        z_ref[slot, h] = _dot(k_ref[0, 0, pl.ds(k0, SB_T), :], qt[h])

    def values(k, h):
        v_t = vt_ref[i - k, h * HEAD_DIM:(h + 1) * HEAD_DIM, :]
        acc_ref[h] += _dot(v_t, w_ref[h])

    def weights(slot, h, c, diag):
        zt = z_ref[slot, h]
        sp = jnp.maximum(zt, 0.0) + jnp.log(1.0 + jnp.exp2(jnp.abs(zt) * -LOG2E))
        if diag:
            sp = jnp.where(causal, sp, 0.0)
        spb = sp.astype(BF16)
        xs = [None] * n_sub
        for sb in reversed(range(n_sub)):
            r0, r1 = sb * SB_SUB, (sb + 1) * SB_SUB
            cs = _dot(tri, spb[r0:r1])
            xs[sb] = (zt[r0:r1] - cs) - c
            c = c + cs[0:1, :]
        w = jnp.exp(jnp.concatenate(xs, axis=0))
        if diag:
            w = jnp.where(causal, w, 0.0)
        w_ref[h] = w.astype(BF16)
        return c

    acc_ref[...] = jnp.zeros_like(acc_ref)
    for k in range(N_SLOT):
        for h in heads:
            scores(k, k, h)
    cs = tuple(weights(0, h, jnp.zeros((1, SB_T), F32), True) for h in heads)

    def step(k, slot, cs, prefetch):
        if prefetch:
            for h in heads:
                scores(k + N_SLOT - 1, (slot + N_SLOT - 1) % N_SLOT, h)
        for h in heads:
            values(k - 1, h)
        return tuple(weights(slot, h, cs[h], False) for h in heads)

    def unrolled(m, cs):
        k = N_SLOT * m + 1
        for u in range(N_SLOT):
            cs = step(k + u, (1 + u) % N_SLOT, cs, True)
        return cs

    n_full = i // N_SLOT
    rem = i - n_full * N_SLOT
    cs = lax.fori_loop(0, n_full, unrolled, cs)
    k_rem = N_SLOT * n_full + 1

    for n_rem in range(1, N_SLOT):
        @pl.when(rem == n_rem)
        def _():
            c = cs
            for u in range(n_rem):
                c = step(k_rem + u, (1 + u) % N_SLOT, c, False)

    for h in heads:
        values(i, h)

    o_ref[0] = jnp.concatenate([acc_ref[h] for h in heads], axis=0).T


def _sb_attention(proj, batch, seq):
    n_pairs = SB_HEADS // HEAD_PAIR
    k_off = SB_W // LANES
    v_off = 2 * SB_W // LANES
    return pl.pallas_call(
        _sb_kernel,
        grid=(batch, n_pairs, seq // SB_T),
        in_specs=[
            pl.BlockSpec((1, 1, SB_T, LANES), lambda b, p, i: (b, p, i, 0)),
            pl.BlockSpec((1, 1, seq, LANES), lambda b, p, i: (b, k_off + p, 0, 0)),
            pl.BlockSpec((1, 1, seq, LANES), lambda b, p, i: (b, v_off + p, 0, 0)),
        ],
        out_specs=pl.BlockSpec((1, SB_T, LANES), lambda b, p, i: (b, i, p)),
        out_shape=jax.ShapeDtypeStruct((batch, seq, SB_W), F32),
        scratch_shapes=[
            pltpu.VMEM((seq // SB_T, LANES, SB_T), BF16),
            pltpu.VMEM((N_SLOT, HEAD_PAIR, SB_T, SB_T), F32),
            pltpu.VMEM((HEAD_PAIR, SB_T, SB_T), BF16),
            pltpu.VMEM((HEAD_PAIR, HEAD_DIM, SB_T), F32),
        ],
        compiler_params=_params(3),
        name="stick_breaking_attention",
    )(proj, proj, proj)


def _swa_kernel(sink_ref, q_ref, k_ref, v_ref, o_ref, vt_ref):
    i = pl.program_id(1)
    n_blk = vt_ref.shape[0]

    @pl.when(i == 0)
    def _():
        def tr(c, _):
            r0 = pl.multiple_of(c * SWA_T, SWA_T)
            for m in range(SWA_KV_W // LANES):
                vt_ref[c, m * LANES:(m + 1) * LANES, :] = (
                    v_ref[0, m, pl.ds(r0, SWA_T), :].astype(F32).T.astype(BF16))
            return 0
        lax.fori_loop(0, n_blk, tr, 0)

    prev = jnp.maximum(i - 1, 0)
    p0 = pl.multiple_of(prev * SWA_T, SWA_T)
    c0 = pl.multiple_of(i * SWA_T, SWA_T)
    kband = [jnp.concatenate([k_ref[0, m, pl.ds(p0, SWA_T), :],
                              k_ref[0, m, pl.ds(c0, SWA_T), :]], axis=0)
             for m in range(SWA_KV_W // LANES)]
    vtband = jnp.concatenate([vt_ref[prev], vt_ref[i]], axis=1)

    band = 2 * SWA_T
    c = lax.broadcasted_iota(jnp.int32, (band, SWA_T), 0)
    r = lax.broadcasted_iota(jnp.int32, (band, SWA_T), 1)
    dist = r + SWA_T - c
    valid = (dist >= 0) & (dist < WINDOW) & ((c >= SWA_T) | (i > 0))
    dist_f = dist.astype(F32)
    neg_inf = jnp.float32(-jnp.inf)

    o_heads = []
    for g in range(SWA_KV_HEADS):
        gs = slice(g * HEAD_DIM, (g + 1) * HEAD_DIM)
        group = range(g * SWA_GROUP, (g + 1) * SWA_GROUP)
        half = lambda n: slice((n % HEAD_PAIR) * HEAD_DIM, (n % HEAD_PAIR + 1) * HEAD_DIM)
        qg = jnp.concatenate(
            [q_ref[0, h // HEAD_PAIR, :, half(h)] for h in group], axis=0) * Q_SCALE
        zt = _dot_nt(kband[g // HEAD_PAIR][:, half(g)], qg)
        ps, dens = [], []
        for n, h in enumerate(group):
            slope = 2.0 ** (-8.0 * (h + 1.0) / SWA_HEADS)
            sink = sink_ref[h]
            s = jnp.where(valid, zt[:, n * SWA_T:(n + 1) * SWA_T] - slope * dist_f, neg_inf)
            mx = jnp.maximum(jnp.max(s, axis=0, keepdims=True), sink)
            p = jnp.exp(s - mx)
            dens.append(jnp.sum(p, axis=0, keepdims=True) + jnp.exp(sink - mx))
            ps.append(p.astype(BF16))
        o_t = _dot(vtband[gs, :], jnp.concatenate(ps, axis=1))
        for n in range(SWA_GROUP):
            o_heads.append(o_t[:, n * SWA_T:(n + 1) * SWA_T] / dens[n])

    for m in range(SWA_HEADS // HEAD_PAIR):
        pair = jnp.concatenate(o_heads[HEAD_PAIR * m:HEAD_PAIR * (m + 1)], axis=0)
        o_ref[0, :, m * LANES:(m + 1) * LANES] = pair.T


def _swa_attention(proj, sinks, batch, seq):
    q_n, kv_n = SWA_Q_W // LANES, SWA_KV_W // LANES
    q_blk = 3 * SB_W // SWA_Q_W
    k_blk = (3 * SB_W + SWA_Q_W) // SWA_KV_W
    v_blk = k_blk + 1
    return pl.pallas_call(
        _swa_kernel,
        grid=(batch, seq // SWA_T),
        in_specs=[
            pl.BlockSpec(memory_space=pltpu.SMEM),
            pl.BlockSpec((1, q_n, SWA_T, LANES), lambda b, i: (b, q_blk, i, 0)),
            pl.BlockSpec((1, kv_n, seq, LANES), lambda b, i: (b, k_blk, 0, 0)),
            pl.BlockSpec((1, kv_n, seq, LANES), lambda b, i: (b, v_blk, 0, 0)),
        ],
        out_specs=pl.BlockSpec((1, SWA_T, SWA_Q_W), lambda b, i: (b, i, 0)),
        out_shape=jax.ShapeDtypeStruct((batch, seq, SWA_Q_W), F32),
        scratch_shapes=[pltpu.VMEM((seq // SWA_T, SWA_KV_W, SWA_T), BF16)],
        compiler_params=_params(2),
        name="sliding_window_attention",
    )(sinks, proj, proj, proj)


def _out_proj_kernel(sb_ref, sw_ref, ga_ref, gb_ref, w_ref, x_ref, o_ref):
    mixed = jnp.concatenate([_rms(sb_ref[...], ga_ref[...]).astype(BF16),
                             _rms(sw_ref[...], gb_ref[...]).astype(BF16)], axis=1)
    o_ref[...] = x_ref[...] + _dot(mixed, w_ref[...].astype(BF16))


def _out_proj(sb_o, sw_o, ga, gb, w, layer, x):
    t, d = x.shape
    mix = SB_W + SWA_Q_W
    return pl.pallas_call(
        _out_proj_kernel,
        grid=(t // OUT_TM,),
        in_specs=[
            pl.BlockSpec((OUT_TM, SB_W), lambda i: (i, 0)),
            pl.BlockSpec((OUT_TM, SWA_Q_W), lambda i: (i, 0)),
            pl.BlockSpec((1, SB_W), lambda i: (0, 0)),
            pl.BlockSpec((1, SWA_Q_W), lambda i: (0, 0)),
            pl.BlockSpec((None, mix, d), lambda i: (layer, 0, 0),
                         pipeline_mode=pl.Buffered(1)),
            pl.BlockSpec((OUT_TM, d), lambda i: (i, 0)),
        ],
        out_specs=pl.BlockSpec((OUT_TM, d), lambda i: (i, 0)),
        out_shape=jax.ShapeDtypeStruct((t, d), F32),
        compiler_params=_params(1),
        name="out_proj",
    )(sb_o, sw_o, ga.reshape(1, SB_W), gb.reshape(1, SWA_Q_W), w, x)


def _ffn_kernel(x_ref, g_ref, wg_ref, wu_ref, wd_ref, gf_ref, o_ref, h_ref, *, final_norm):
    f = pl.program_id(1)

    @pl.when(f == 0)
    def _():
        x = x_ref[...]
        h_ref[...] = _rms(x, g_ref[...]).astype(BF16)
        o_ref[...] = x

    h = h_ref[...]
    gate = _dot(h, wg_ref[...].astype(BF16))
    up = _dot(h, wu_ref[...].astype(BF16))
    act = (gate * jax.nn.sigmoid(gate) * up).astype(BF16)
    o_ref[...] += _dot(act, wd_ref[...].astype(BF16))

    if final_norm:
        @pl.when(f == pl.num_programs(1) - 1)
        def _():
            o_ref[...] = _rms(o_ref[...], gf_ref[...])


def _ffn(x, g, w_gate_up, w_down, layer, g_final, final_norm):
    t, d = x.shape
    n_f = D_FF // FFN_TF
    return pl.pallas_call(
        functools.partial(_ffn_kernel, final_norm=final_norm),
        grid=(t // FFN_TM, n_f),
        in_specs=[
            pl.BlockSpec((FFN_TM, d), lambda i, f: (i, 0)),
            pl.BlockSpec((1, d), lambda i, f: (0, 0)),
            pl.BlockSpec((None, d, FFN_TF), lambda i, f: (layer, 0, f)),
            pl.BlockSpec((None, d, FFN_TF), lambda i, f: (layer, 0, f + n_f)),
            pl.BlockSpec((None, FFN_TF, d), lambda i, f: (layer, f, 0)),
            pl.BlockSpec((1, d), lambda i, f: (0, 0)),
        ],
        out_specs=pl.BlockSpec((FFN_TM, d), lambda i, f: (i, 0)),
        out_shape=jax.ShapeDtypeStruct((t, d), F32),
        scratch_shapes=[pltpu.VMEM((FFN_TM, d), BF16)],
        compiler_params=_params(2),
        name="swiglu_ffn",
    )(x, g.reshape(1, d), w_gate_up, w_gate_up, w_down, g_final.reshape(1, d))


def kernel(x, ln_mix, w_in, sb_out_norm, swa_out_norm, swa_sinks, w_out, ln_ffn, w_gate_up, w_down, ln_final):
    batch, seq, d = x.shape
    depth = w_in.shape[0]
    xt = x.reshape(batch * seq, d)
    for l in range(depth):
        proj = _norm_proj(xt, ln_mix[l], w_in, l, batch, seq)
        sb_o = _sb_attention(proj, batch, seq).reshape(batch * seq, SB_W)
        sw_o = _swa_attention(proj, swa_sinks[l], batch, seq).reshape(batch * seq, SWA_Q_W)
        xt = _out_proj(sb_o, sw_o, sb_out_norm[l], swa_out_norm[l], w_out, l, xt)
        xt = _ffn(xt, ln_ffn[l], w_gate_up, w_down, l, ln_final, final_norm=(l == depth - 1))
    return xt.reshape(batch, seq, d)
```

```python
import functools

import jax
import jax.numpy as jnp
from jax import lax
from jax.experimental import pallas as pl
from jax.experimental.pallas import tpu as pltpu

D_MODEL = 2048
HEAD_DIM = 64
SB_HEADS = 16
SWA_HEADS = 16
SWA_KV_HEADS = 4
SWA_GROUP = SWA_HEADS // SWA_KV_HEADS
WINDOW = 128
D_FF = 5632
RMS_EPS = 1e-5

SB_W = SB_HEADS * HEAD_DIM
SWA_Q_W = SWA_HEADS * HEAD_DIM
SWA_KV_W = SWA_KV_HEADS * HEAD_DIM
IN_COLS = 3 * SB_W + SWA_Q_W + 2 * SWA_KV_W
Q_SCALE = HEAD_DIM ** -0.5
LOG2E = 1.4426950408889634

LANES = 128
HEAD_PAIR = LANES // HEAD_DIM
VMEM_LIMIT = 56 * 1024 * 1024

F32 = jnp.float32
BF16 = jnp.bfloat16

PROJ_TM, PROJ_TN = 1024, 768
SB_T = 256
SB_SUB = 128
SB_QB = 2
N_SLOT = 3
SWA_T = WINDOW
OUT_TM = 512
FFN_TM, FFN_TF = 1024, 256


def _params(n_axes):
    return pltpu.CompilerParams(
        dimension_semantics=("arbitrary",) * n_axes,
        vmem_limit_bytes=VMEM_LIMIT)


def _rms(x, g):
    ms = jnp.mean(x * x, axis=-1, keepdims=True)
    return x * lax.rsqrt(ms + RMS_EPS) * g


def _dot(a, b):
    return jnp.dot(a, b, preferred_element_type=F32)


def _dot_nt(a, b):
    return lax.dot_general(a, b, (((1,), (1,)), ((), ())),
                           preferred_element_type=F32)


def _norm_proj_kernel(x_ref, g_ref, w_ref, o_ref, h_ref):
    @pl.when(pl.program_id(1) == 0)
    def _():
        h_ref[...] = _rms(x_ref[...], g_ref[...]).astype(BF16)

    res = _dot(h_ref[...], w_ref[...].astype(BF16))
    for c in range(PROJ_TN // LANES):
        o_ref[0, c] = res[:, c * LANES:(c + 1) * LANES].astype(o_ref.dtype)


def _norm_proj(x, g, w, layer, batch, seq):
    t, d = x.shape
    n = w.shape[2]
    tiles_per_seq = seq // PROJ_TM
    blk_per_tile = PROJ_TN // LANES
    return pl.pallas_call(
        _norm_proj_kernel,
        grid=(t // PROJ_TM, n // PROJ_TN),
        in_specs=[
            pl.BlockSpec((PROJ_TM, d), lambda i, j: (i, 0)),
            pl.BlockSpec((1, d), lambda i, j: (0, 0)),
            pl.BlockSpec((None, d, PROJ_TN), lambda i, j: (layer, 0, j)),
        ],
        out_specs=pl.BlockSpec((1, blk_per_tile, PROJ_TM, LANES),
                               lambda i, j: (i // tiles_per_seq, j, i % tiles_per_seq, 0)),
        out_shape=jax.ShapeDtypeStruct((batch, n // LANES, seq, LANES), BF16),
        scratch_shapes=[pltpu.VMEM((PROJ_TM, d), BF16)],
        compiler_params=_params(2),
        name="norm_in_proj",
    )(x, g.reshape(1, d), w)


def _sb_kernel(q_ref, k_ref, v_ref, o_ref, vt_ref, z_ref, w_ref, acc_ref):
    grid_step = pl.program_id(2)
    n_kblk = vt_ref.shape[0]

    @pl.when(grid_step == 0)
    def _():
        def tr(c, _):
            r0 = pl.multiple_of(c * SB_T, SB_T)
            vt_ref[c] = v_ref[0, 0, pl.ds(r0, SB_T), :].astype(F32).T.astype(BF16)
            return 0
        lax.fori_loop(0, n_kblk, tr, 0)

    r = lax.broadcasted_iota(jnp.int32, (SB_SUB, SB_SUB), 0)
    j2 = lax.broadcasted_iota(jnp.int32, (SB_SUB, SB_SUB), 1)
    tri = jnp.where(j2 >= r, 1.0, 0.0).astype(BF16)

    srow = lax.broadcasted_iota(jnp.int32, (SB_T, SB_T), 0)
    tcol = lax.broadcasted_iota(jnp.int32, (SB_T, SB_T), 1)
    causal = srow < tcol

    n_sub = SB_T // SB_SUB
    heads = range(HEAD_PAIR)

    def q_block(i, rows):
        q_t = (q_ref[0, 0, rows, :].astype(F32) * Q_SCALE).T
        feat = lax.broadcasted_iota(jnp.int32, q_t.shape, 0)
        qt = [jnp.where((feat // HEAD_DIM) == h, q_t, 0.0).astype(BF16) for h in heads]

        def scores(k, slot, h):
            j = jnp.maximum(i - k, 0)
            k0 = pl.multiple_of(j * SB_T, SB_T)
            z_ref[slot, h] = _dot(k_ref[0, 0, pl.ds(k0, SB_T), :], qt[h])

        def values(k, h):
            v_t = vt_ref[i - k, h * HEAD_DIM:(h + 1) * HEAD_DIM, :]
            acc_ref[h] += _dot(v_t, w_ref[h])

        def weights(slot, h, c, diag):
            zt = z_ref[slot, h]
            sp = jnp.maximum(zt, 0.0) + jnp.log(1.0 + jnp.exp2(jnp.abs(zt) * -LOG2E))
            if diag:
                sp = jnp.where(causal, sp, 0.0)
            spb = sp.astype(BF16)
            xs = [None] * n_sub
            for sb in reversed(range(n_sub)):
                r0, r1 = sb * SB_SUB, (sb + 1) * SB_SUB
                cs = _dot(tri, spb[r0:r1])
                xs[sb] = (zt[r0:r1] - cs) - c
                c = c + cs[0:1, :]
            w = jnp.exp(jnp.concatenate(xs, axis=0))
            if diag:
                w = jnp.where(causal, w, 0.0)
            w_ref[h] = w.astype(BF16)
            return c

        acc_ref[...] = jnp.zeros_like(acc_ref)
        for k in range(N_SLOT):
            for h in heads:
                scores(k, k, h)
        cs = tuple(weights(0, h, jnp.zeros((1, SB_T), F32), True) for h in heads)

        def step(k, slot, cs, prefetch):
            if prefetch:
                for h in heads:
                    scores(k + N_SLOT - 1, (slot + N_SLOT - 1) % N_SLOT, h)
            for h in heads:
                values(k - 1, h)
            return tuple(weights(slot, h, cs[h], False) for h in heads)

        def unrolled(m, cs):
            k = N_SLOT * m + 1
            for u in range(N_SLOT):
                cs = step(k + u, (1 + u) % N_SLOT, cs, True)
            return cs

        n_full = i // N_SLOT
        rem = i - n_full * N_SLOT
        cs = lax.fori_loop(0, n_full, unrolled, cs)
        k_rem = N_SLOT * n_full + 1

        for n_rem in range(1, N_SLOT):
            @pl.when(rem == n_rem)
            def _():
                c = cs
                for u in range(n_rem):
                    c = step(k_rem + u, (1 + u) % N_SLOT, c, False)

        for h in heads:
            values(i, h)

        o_ref[0, rows, :] = jnp.concatenate([acc_ref[h] for h in heads], axis=0).T

    for sub in range(SB_QB):
        q_block(grid_step * SB_QB + sub, slice(sub * SB_T, (sub + 1) * SB_T))


def _sb_attention(proj, batch, seq):
    n_pairs = SB_HEADS // HEAD_PAIR
    k_off = SB_W // LANES
    v_off = 2 * SB_W // LANES
    return pl.pallas_call(
        _sb_kernel,
        grid=(batch, n_pairs, seq // (SB_T * SB_QB)),
        in_specs=[
            pl.BlockSpec((1, 1, SB_T * SB_QB, LANES), lambda b, p, i: (b, p, i, 0)),
            pl.BlockSpec((1, 1, seq, LANES), lambda b, p, i: (b, k_off + p, 0, 0)),
            pl.BlockSpec((1, 1, seq, LANES), lambda b, p, i: (b, v_off + p, 0, 0)),
        ],
        out_specs=pl.BlockSpec((1, SB_T * SB_QB, LANES), lambda b, p, i: (b, i, p)),
        out_shape=jax.ShapeDtypeStruct((batch, seq, SB_W), F32),
        scratch_shapes=[
            pltpu.VMEM((seq // SB_T, LANES, SB_T), BF16),
            pltpu.VMEM((N_SLOT, HEAD_PAIR, SB_T, SB_T), F32),
            pltpu.VMEM((HEAD_PAIR, SB_T, SB_T), BF16),
            pltpu.VMEM((HEAD_PAIR, HEAD_DIM, SB_T), F32),
        ],
        compiler_params=_params(3),
        name="stick_breaking_attention",
    )(proj, proj, proj)


def _swa_kernel(sink_ref, q_ref, k_ref, v_ref, o_ref, vt_ref):
    i = pl.program_id(1)
    n_blk = vt_ref.shape[0]

    @pl.when(i == 0)
    def _():
        def tr(c, _):
            r0 = pl.multiple_of(c * SWA_T, SWA_T)
            for m in range(SWA_KV_W // LANES):
                vt_ref[c, m * LANES:(m + 1) * LANES, :] = (
                    v_ref[0, m, pl.ds(r0, SWA_T), :].astype(F32).T.astype(BF16))
            return 0
        lax.fori_loop(0, n_blk, tr, 0)

    prev = jnp.maximum(i - 1, 0)
    p0 = pl.multiple_of(prev * SWA_T, SWA_T)
    c0 = pl.multiple_of(i * SWA_T, SWA_T)
    kband = [jnp.concatenate([k_ref[0, m, pl.ds(p0, SWA_T), :],
                              k_ref[0, m, pl.ds(c0, SWA_T), :]], axis=0)
             for m in range(SWA_KV_W // LANES)]
    vtband = jnp.concatenate([vt_ref[prev], vt_ref[i]], axis=1)

    band = 2 * SWA_T
    c = lax.broadcasted_iota(jnp.int32, (band, SWA_T), 0)
    r = lax.broadcasted_iota(jnp.int32, (band, SWA_T), 1)
    dist = r + SWA_T - c
    valid = (dist >= 0) & (dist < WINDOW) & ((c >= SWA_T) | (i > 0))
    dist_f = dist.astype(F32)
    neg_inf = jnp.float32(-jnp.inf)

    o_heads = []
    for g in range(SWA_KV_HEADS):
        gs = slice(g * HEAD_DIM, (g + 1) * HEAD_DIM)
        group = range(g * SWA_GROUP, (g + 1) * SWA_GROUP)
        half = lambda n: slice((n % HEAD_PAIR) * HEAD_DIM, (n % HEAD_PAIR + 1) * HEAD_DIM)
        qg = jnp.concatenate(
            [q_ref[0, h // HEAD_PAIR, :, half(h)] for h in group], axis=0) * Q_SCALE
        zt = _dot_nt(kband[g // HEAD_PAIR][:, half(g)], qg)
        ps, dens = [], []
        for n, h in enumerate(group):
            slope = 2.0 ** (-8.0 * (h + 1.0) / SWA_HEADS)
            sink = sink_ref[h]
            s = jnp.where(valid, zt[:, n * SWA_T:(n + 1) * SWA_T] - slope * dist_f, neg_inf)
            mx = jnp.maximum(jnp.max(s, axis=0, keepdims=True), sink)
            p = jnp.exp(s - mx)
            dens.append(jnp.sum(p, axis=0, keepdims=True) + jnp.exp(sink - mx))
            ps.append(p.astype(BF16))
        o_t = _dot(vtband[gs, :], jnp.concatenate(ps, axis=1))
        for n in range(SWA_GROUP):
            o_heads.append(o_t[:, n * SWA_T:(n + 1) * SWA_T] / dens[n])

    for m in range(SWA_HEADS // HEAD_PAIR):
        pair = jnp.concatenate(o_heads[HEAD_PAIR * m:HEAD_PAIR * (m + 1)], axis=0)
        o_ref[0, :, m * LANES:(m + 1) * LANES] = pair.T


def _swa_attention(proj, sinks, batch, seq):
    q_n, kv_n = SWA_Q_W // LANES, SWA_KV_W // LANES
    q_blk = 3 * SB_W // SWA_Q_W
    k_blk = (3 * SB_W + SWA_Q_W) // SWA_KV_W
    v_blk = k_blk + 1
    return pl.pallas_call(
        _swa_kernel,
        grid=(batch, seq // SWA_T),
        in_specs=[
            pl.BlockSpec(memory_space=pltpu.SMEM),
            pl.BlockSpec((1, q_n, SWA_T, LANES), lambda b, i: (b, q_blk, i, 0)),
            pl.BlockSpec((1, kv_n, seq, LANES), lambda b, i: (b, k_blk, 0, 0)),
            pl.BlockSpec((1, kv_n, seq, LANES), lambda b, i: (b, v_blk, 0, 0)),
        ],
        out_specs=pl.BlockSpec((1, SWA_T, SWA_Q_W), lambda b, i: (b, i, 0)),
        out_shape=jax.ShapeDtypeStruct((batch, seq, SWA_Q_W), F32),
        scratch_shapes=[pltpu.VMEM((seq // SWA_T, SWA_KV_W, SWA_T), BF16)],
        compiler_params=_params(2),
        name="sliding_window_attention",
    )(sinks, proj, proj, proj)


def _out_proj_kernel(sb_ref, sw_ref, ga_ref, gb_ref, w_ref, x_ref, o_ref):
    mixed = jnp.concatenate([_rms(sb_ref[...], ga_ref[...]).astype(BF16),
                             _rms(sw_ref[...], gb_ref[...]).astype(BF16)], axis=1)
    o_ref[...] = x_ref[...] + _dot(mixed, w_ref[...].astype(BF16))


def _out_proj(sb_o, sw_o, ga, gb, w, layer, x):
    t, d = x.shape
    mix = SB_W + SWA_Q_W
    return pl.pallas_call(
        _out_proj_kernel,
        grid=(t // OUT_TM,),
        in_specs=[
            pl.BlockSpec((OUT_TM, SB_W), lambda i: (i, 0)),
            pl.BlockSpec((OUT_TM, SWA_Q_W), lambda i: (i, 0)),
            pl.BlockSpec((1, SB_W), lambda i: (0, 0)),
            pl.BlockSpec((1, SWA_Q_W), lambda i: (0, 0)),
            pl.BlockSpec((None, mix, d), lambda i: (layer, 0, 0),
                         pipeline_mode=pl.Buffered(1)),
            pl.BlockSpec((OUT_TM, d), lambda i: (i, 0)),
        ],
        out_specs=pl.BlockSpec((OUT_TM, d), lambda i: (i, 0)),
        out_shape=jax.ShapeDtypeStruct((t, d), F32),
        compiler_params=_params(1),
        name="out_proj",
    )(sb_o, sw_o, ga.reshape(1, SB_W), gb.reshape(1, SWA_Q_W), w, x)


def _ffn_kernel(x_ref, g_ref, wg_ref, wu_ref, wd_ref, gf_ref, o_ref, h_ref, *, final_norm):
    f = pl.program_id(1)

    @pl.when(f == 0)
    def _():
        x = x_ref[...]
        h_ref[...] = _rms(x, g_ref[...]).astype(BF16)
        o_ref[...] = x

    h = h_ref[...]
    gate = _dot(h, wg_ref[...].astype(BF16))
    up = _dot(h, wu_ref[...].astype(BF16))
    act = (gate * jax.nn.sigmoid(gate) * up).astype(BF16)
    o_ref[...] += _dot(act, wd_ref[...].astype(BF16))

    if final_norm:
        @pl.when(f == pl.num_programs(1) - 1)
        def _():
            o_ref[...] = _rms(o_ref[...], gf_ref[...])


def _ffn(x, g, w_gate_up, w_down, layer, g_final, final_norm):
    t, d = x.shape
    n_f = D_FF // FFN_TF
    return pl.pallas_call(
        functools.partial(_ffn_kernel, final_norm=final_norm),
        grid=(t // FFN_TM, n_f),
        in_specs=[
            pl.BlockSpec((FFN_TM, d), lambda i, f: (i, 0)),
            pl.BlockSpec((1, d), lambda i, f: (0, 0)),
            pl.BlockSpec((None, d, FFN_TF), lambda i, f: (layer, 0, f)),
            pl.BlockSpec((None, d, FFN_TF), lambda i, f: (layer, 0, f + n_f)),
            pl.BlockSpec((None, FFN_TF, d), lambda i, f: (layer, f, 0)),
            pl.BlockSpec((1, d), lambda i, f: (0, 0)),
        ],
        out_specs=pl.BlockSpec((FFN_TM, d), lambda i, f: (i, 0)),
        out_shape=jax.ShapeDtypeStruct((t, d), F32),
        scratch_shapes=[pltpu.VMEM((FFN_TM, d), BF16)],
        compiler_params=_params(2),
        name="swiglu_ffn",
    )(x, g.reshape(1, d), w_gate_up, w_gate_up, w_down, g_final.reshape(1, d))


def kernel(x, ln_mix, w_in, sb_out_norm, swa_out_norm, swa_sinks, w_out, ln_ffn, w_gate_up, w_down, ln_final):
    batch, seq, d = x.shape
    depth = w_in.shape[0]
    xt = x.reshape(batch * seq, d)
    for l in range(depth):
        proj = _norm_proj(xt, ln_mix[l], w_in, l, batch, seq)
        sb_o = _sb_attention(proj, batch, seq).reshape(batch * seq, SB_W)
        sw_o = _swa_attention(proj, swa_sinks[l], batch, seq).reshape(batch * seq, SWA_Q_W)
        xt = _out_proj(sb_o, sw_o, sb_out_norm[l], swa_out_norm[l], w_out, l, xt)
        xt = _ffn(xt, ln_ffn[l], w_gate_up, w_down, l, ln_final, final_norm=(l == depth - 1))
    return xt.reshape(batch, seq, d)
```

```python
import functools

import jax
import jax.numpy as jnp
from jax import lax
from jax.experimental import pallas as pl
from jax.experimental.pallas import tpu as pltpu

D_MODEL = 2048
HEAD_DIM = 64
SB_HEADS = 16
SWA_HEADS = 16
SWA_KV_HEADS = 4
SWA_GROUP = SWA_HEADS // SWA_KV_HEADS
WINDOW = 128
D_FF = 5632
RMS_EPS = 1e-5

SB_W = SB_HEADS * HEAD_DIM
SWA_Q_W = SWA_HEADS * HEAD_DIM
SWA_KV_W = SWA_KV_HEADS * HEAD_DIM
IN_COLS = 3 * SB_W + SWA_Q_W + 2 * SWA_KV_W
Q_SCALE = HEAD_DIM ** -0.5
LOG2E = 1.4426950408889634

LANES = 128
HEAD_PAIR = LANES // HEAD_DIM
VMEM_LIMIT = 56 * 1024 * 1024

F32 = jnp.float32
BF16 = jnp.bfloat16

PROJ_TM, PROJ_TN = 1024, 768
SB_T = 256
SB_SUB = 128
SB_QB = 4
N_SLOT = 3
SWA_T = WINDOW
OUT_TM = 512
FFN_TM, FFN_TF = 1024, 256


def _params(n_axes):
    return pltpu.CompilerParams(
        dimension_semantics=("arbitrary",) * n_axes,
        vmem_limit_bytes=VMEM_LIMIT)


def _rms(x, g):
    ms = jnp.mean(x * x, axis=-1, keepdims=True)
    return x * lax.rsqrt(ms + RMS_EPS) * g


def _dot(a, b):
    return jnp.dot(a, b, preferred_element_type=F32)


def _dot_nt(a, b):
    return lax.dot_general(a, b, (((1,), (1,)), ((), ())),
                           preferred_element_type=F32)


def _norm_proj_kernel(x_ref, g_ref, w_ref, o_ref, h_ref):
    @pl.when(pl.program_id(1) == 0)
    def _():
        h_ref[...] = _rms(x_ref[...], g_ref[...]).astype(BF16)

    res = _dot(h_ref[...], w_ref[...].astype(BF16))
    for c in range(PROJ_TN // LANES):
        o_ref[0, c] = res[:, c * LANES:(c + 1) * LANES].astype(o_ref.dtype)


def _norm_proj(x, g, w, layer, batch, seq):
    t, d = x.shape
    n = w.shape[2]
    tiles_per_seq = seq // PROJ_TM
    blk_per_tile = PROJ_TN // LANES
    return pl.pallas_call(
        _norm_proj_kernel,
        grid=(t // PROJ_TM, n // PROJ_TN),
        in_specs=[
            pl.BlockSpec((PROJ_TM, d), lambda i, j: (i, 0)),
            pl.BlockSpec((1, d), lambda i, j: (0, 0)),
            pl.BlockSpec((None, d, PROJ_TN), lambda i, j: (layer, 0, j)),
        ],
        out_specs=pl.BlockSpec((1, blk_per_tile, PROJ_TM, LANES),
                               lambda i, j: (i // tiles_per_seq, j, i % tiles_per_seq, 0)),
        out_shape=jax.ShapeDtypeStruct((batch, n // LANES, seq, LANES), BF16),
        scratch_shapes=[pltpu.VMEM((PROJ_TM, d), BF16)],
        compiler_params=_params(2),
        name="norm_in_proj",
    )(x, g.reshape(1, d), w)


def _sb_kernel(q_ref, k_ref, v_ref, o_ref, vt_ref, z_ref, w_ref, acc_ref):
    grid_step = pl.program_id(2)
    n_kblk = vt_ref.shape[0]

    @pl.when(grid_step == 0)
    def _():
        def tr(c, _):
            r0 = pl.multiple_of(c * SB_T, SB_T)
            vt_ref[c] = v_ref[0, 0, pl.ds(r0, SB_T), :].astype(F32).T.astype(BF16)
            return 0
        lax.fori_loop(0, n_kblk, tr, 0)

    r = lax.broadcasted_iota(jnp.int32, (SB_SUB, SB_SUB), 0)
    j2 = lax.broadcasted_iota(jnp.int32, (SB_SUB, SB_SUB), 1)
    tri = jnp.where(j2 >= r, 1.0, 0.0).astype(BF16)

    srow = lax.broadcasted_iota(jnp.int32, (SB_T, SB_T), 0)
    tcol = lax.broadcasted_iota(jnp.int32, (SB_T, SB_T), 1)
    causal = srow < tcol

    n_sub = SB_T // SB_SUB
    heads = range(HEAD_PAIR)

    def q_block(i, rows):
        q_t = (q_ref[0, 0, rows, :].astype(F32) * Q_SCALE).T
        feat = lax.broadcasted_iota(jnp.int32, q_t.shape, 0)
        qt = [jnp.where((feat // HEAD_DIM) == h, q_t, 0.0).astype(BF16) for h in heads]

        def scores(k, slot, h):
            j = jnp.maximum(i - k, 0)
            k0 = pl.multiple_of(j * SB_T, SB_T)
            z_ref[slot, h] = _dot(k_ref[0, 0, pl.ds(k0, SB_T), :], qt[h])

        def values(k, h):
            v_t = vt_ref[i - k, h * HEAD_DIM:(h + 1) * HEAD_DIM, :]
            acc_ref[h] += _dot(v_t, w_ref[h])

        def weights(slot, h, c, diag):
            zt = z_ref[slot, h]
            sp = jnp.maximum(zt, 0.0) + jnp.log(1.0 + jnp.exp2(jnp.abs(zt) * -LOG2E))
            if diag:
                sp = jnp.where(causal, sp, 0.0)
            spb = sp.astype(BF16)
            xs = [None] * n_sub
            for sb in reversed(range(n_sub)):
                r0, r1 = sb * SB_SUB, (sb + 1) * SB_SUB
                cs = _dot(tri, spb[r0:r1])
                xs[sb] = (zt[r0:r1] - cs) - c
                c = c + cs[0:1, :]
            w = jnp.exp(jnp.concatenate(xs, axis=0))
            if diag:
                w = jnp.where(causal, w, 0.0)
            w_ref[h] = w.astype(BF16)
            return c

        acc_ref[...] = jnp.zeros_like(acc_ref)
        for k in range(N_SLOT):
            for h in heads:
                scores(k, k, h)
        cs = tuple(weights(0, h, jnp.zeros((1, SB_T), F32), True) for h in heads)

        def step(k, slot, cs, prefetch):
            if prefetch:
                for h in heads:
                    scores(k + N_SLOT - 1, (slot + N_SLOT - 1) % N_SLOT, h)
            for h in heads:
                values(k - 1, h)
            return tuple(weights(slot, h, cs[h], False) for h in heads)

        def unrolled(m, cs):
            k = N_SLOT * m + 1
            for u in range(N_SLOT):
                cs = step(k + u, (1 + u) % N_SLOT, cs, True)
            return cs

        n_full = i // N_SLOT
        rem = i - n_full * N_SLOT
        cs = lax.fori_loop(0, n_full, unrolled, cs)
        k_rem = N_SLOT * n_full + 1

        for n_rem in range(1, N_SLOT):
            @pl.when(rem == n_rem)
            def _():
                c = cs
                for u in range(n_rem):
                    c = step(k_rem + u, (1 + u) % N_SLOT, c, False)

        for h in heads:
            values(i, h)

        o_ref[0, rows, :] = jnp.concatenate([acc_ref[h] for h in heads], axis=0).T

    for sub in range(SB_QB):
        q_block(grid_step * SB_QB + sub, slice(sub * SB_T, (sub + 1) * SB_T))


def _sb_attention(proj, batch, seq):
    n_pairs = SB_HEADS // HEAD_PAIR
    k_off = SB_W // LANES
    v_off = 2 * SB_W // LANES
    return pl.pallas_call(
        _sb_kernel,
        grid=(batch, n_pairs, seq // (SB_T * SB_QB)),
        in_specs=[
            pl.BlockSpec((1, 1, SB_T * SB_QB, LANES), lambda b, p, i: (b, p, i, 0)),
            pl.BlockSpec((1, 1, seq, LANES), lambda b, p, i: (b, k_off + p, 0, 0)),
            pl.BlockSpec((1, 1, seq, LANES), lambda b, p, i: (b, v_off + p, 0, 0)),
        ],
        out_specs=pl.BlockSpec((1, SB_T * SB_QB, LANES), lambda b, p, i: (b, i, p)),
        out_shape=jax.ShapeDtypeStruct((batch, seq, SB_W), F32),
        scratch_shapes=[
            pltpu.VMEM((seq // SB_T, LANES, SB_T), BF16),
            pltpu.VMEM((N_SLOT, HEAD_PAIR, SB_T, SB_T), F32),
            pltpu.VMEM((HEAD_PAIR, SB_T, SB_T), BF16),
            pltpu.VMEM((HEAD_PAIR, HEAD_DIM, SB_T), F32),
        ],
        compiler_params=_params(3),
        name="stick_breaking_attention",
    )(proj, proj, proj)


def _swa_kernel(sink_ref, q_ref, k_ref, v_ref, o_ref, vt_ref):
    i = pl.program_id(1)
    n_blk = vt_ref.shape[0]

    @pl.when(i == 0)
    def _():
        def tr(c, _):
            r0 = pl.multiple_of(c * SWA_T, SWA_T)
            for m in range(SWA_KV_W // LANES):
                vt_ref[c, m * LANES:(m + 1) * LANES, :] = (
                    v_ref[0, m, pl.ds(r0, SWA_T), :].astype(F32).T.astype(BF16))
            return 0
        lax.fori_loop(0, n_blk, tr, 0)

    prev = jnp.maximum(i - 1, 0)
    p0 = pl.multiple_of(prev * SWA_T, SWA_T)
    c0 = pl.multiple_of(i * SWA_T, SWA_T)
    kband = [jnp.concatenate([k_ref[0, m, pl.ds(p0, SWA_T), :],
                              k_ref[0, m, pl.ds(c0, SWA_T), :]], axis=0)
             for m in range(SWA_KV_W // LANES)]
    vtband = jnp.concatenate([vt_ref[prev], vt_ref[i]], axis=1)

    band = 2 * SWA_T
    c = lax.broadcasted_iota(jnp.int32, (band, SWA_T), 0)
    r = lax.broadcasted_iota(jnp.int32, (band, SWA_T), 1)
    dist = r + SWA_T - c
    valid = (dist >= 0) & (dist < WINDOW) & ((c >= SWA_T) | (i > 0))
    dist_f = dist.astype(F32)
    neg_inf = jnp.float32(-jnp.inf)

    o_heads = []
    for g in range(SWA_KV_HEADS):
        gs = slice(g * HEAD_DIM, (g + 1) * HEAD_DIM)
        group = range(g * SWA_GROUP, (g + 1) * SWA_GROUP)
        half = lambda n: slice((n % HEAD_PAIR) * HEAD_DIM, (n % HEAD_PAIR + 1) * HEAD_DIM)
        qg = jnp.concatenate(
            [q_ref[0, h // HEAD_PAIR, :, half(h)] for h in group], axis=0) * Q_SCALE
        zt = _dot_nt(kband[g // HEAD_PAIR][:, half(g)], qg)
        ps, dens = [], []
        for n, h in enumerate(group):
            slope = 2.0 ** (-8.0 * (h + 1.0) / SWA_HEADS)
            sink = sink_ref[h]
            s = jnp.where(valid, zt[:, n * SWA_T:(n + 1) * SWA_T] - slope * dist_f, neg_inf)
            mx = jnp.maximum(jnp.max(s, axis=0, keepdims=True), sink)
            p = jnp.exp(s - mx)
            dens.append(jnp.sum(p, axis=0, keepdims=True) + jnp.exp(sink - mx))
            ps.append(p.astype(BF16))
        o_t = _dot(vtband[gs, :], jnp.concatenate(ps, axis=1))
        for n in range(SWA_GROUP):
            o_heads.append(o_t[:, n * SWA_T:(n + 1) * SWA_T] / dens[n])

    for m in range(SWA_HEADS // HEAD_PAIR):
        pair = jnp.concatenate(o_heads[HEAD_PAIR * m:HEAD_PAIR * (m + 1)], axis=0)
        o_ref[0, :, m * LANES:(m + 1) * LANES] = pair.T


def _swa_attention(proj, sinks, batch, seq):
    q_n, kv_n = SWA_Q_W // LANES, SWA_KV_W // LANES
    q_blk = 3 * SB_W // SWA_Q_W
    k_blk = (3 * SB_W + SWA_Q_W) // SWA_KV_W
    v_blk = k_blk + 1
    return pl.pallas_call(
        _swa_kernel,
        grid=(batch, seq // SWA_T),
        in_specs=[
            pl.BlockSpec(memory_space=pltpu.SMEM),
            pl.BlockSpec((1, q_n, SWA_T, LANES), lambda b, i: (b, q_blk, i, 0)),
            pl.BlockSpec((1, kv_n, seq, LANES), lambda b, i: (b, k_blk, 0, 0)),
            pl.BlockSpec((1, kv_n, seq, LANES), lambda b, i: (b, v_blk, 0, 0)),
        ],
        out_specs=pl.BlockSpec((1, SWA_T, SWA_Q_W), lambda b, i: (b, i, 0)),
        out_shape=jax.ShapeDtypeStruct((batch, seq, SWA_Q_W), F32),
        scratch_shapes=[pltpu.VMEM((seq // SWA_T, SWA_KV_W, SWA_T), BF16)],
        compiler_params=_params(2),
        name="sliding_window_attention",
    )(sinks, proj, proj, proj)


def _out_proj_kernel(sb_ref, sw_ref, ga_ref, gb_ref, w_ref, x_ref, o_ref):
    mixed = jnp.concatenate([_rms(sb_ref[...], ga_ref[...]).astype(BF16),
                             _rms(sw_ref[...], gb_ref[...]).astype(BF16)], axis=1)
    o_ref[...] = x_ref[...] + _dot(mixed, w_ref[...].astype(BF16))


def _out_proj(sb_o, sw_o, ga, gb, w, layer, x):
    t, d = x.shape
    mix = SB_W + SWA_Q_W
    return pl.pallas_call(
        _out_proj_kernel,
        grid=(t // OUT_TM,),
        in_specs=[
            pl.BlockSpec((OUT_TM, SB_W), lambda i: (i, 0)),
            pl.BlockSpec((OUT_TM, SWA_Q_W), lambda i: (i, 0)),
            pl.BlockSpec((1, SB_W), lambda i: (0, 0)),
            pl.BlockSpec((1, SWA_Q_W), lambda i: (0, 0)),
            pl.BlockSpec((None, mix, d), lambda i: (layer, 0, 0),
                         pipeline_mode=pl.Buffered(1)),
            pl.BlockSpec((OUT_TM, d), lambda i: (i, 0)),
        ],
        out_specs=pl.BlockSpec((OUT_TM, d), lambda i: (i, 0)),
        out_shape=jax.ShapeDtypeStruct((t, d), F32),
        compiler_params=_params(1),
        name="out_proj",
    )(sb_o, sw_o, ga.reshape(1, SB_W), gb.reshape(1, SWA_Q_W), w, x)


def _ffn_kernel(x_ref, g_ref, wg_ref, wu_ref, wd_ref, gf_ref, o_ref, h_ref, *, final_norm):
    f = pl.program_id(1)

    @pl.when(f == 0)
    def _():
        x = x_ref[...]
        h_ref[...] = _rms(x, g_ref[...]).astype(BF16)
        o_ref[...] = x

    h = h_ref[...]
    gate = _dot(h, wg_ref[...].astype(BF16))
    up = _dot(h, wu_ref[...].astype(BF16))
    act = (gate * jax.nn.sigmoid(gate) * up).astype(BF16)
    o_ref[...] += _dot(act, wd_ref[...].astype(BF16))

    if final_norm:
        @pl.when(f == pl.num_programs(1) - 1)
        def _():
            o_ref[...] = _rms(o_ref[...], gf_ref[...])


def _ffn(x, g, w_gate_up, w_down, layer, g_final, final_norm):
    t, d = x.shape
    n_f = D_FF // FFN_TF
    return pl.pallas_call(
        functools.partial(_ffn_kernel, final_norm=final_norm),
        grid=(t // FFN_TM, n_f),
        in_specs=[
            pl.BlockSpec((FFN_TM, d), lambda i, f: (i, 0)),
            pl.BlockSpec((1, d), lambda i, f: (0, 0)),
            pl.BlockSpec((None, d, FFN_TF), lambda i, f: (layer, 0, f)),
            pl.BlockSpec((None, d, FFN_TF), lambda i, f: (layer, 0, f + n_f)),
            pl.BlockSpec((None, FFN_TF, d), lambda i, f: (layer, f, 0)),
            pl.BlockSpec((1, d), lambda i, f: (0, 0)),
        ],
        out_specs=pl.BlockSpec((FFN_TM, d), lambda i, f: (i, 0)),
        out_shape=jax.ShapeDtypeStruct((t, d), F32),
        scratch_shapes=[pltpu.VMEM((FFN_TM, d), BF16)],
        compiler_params=_params(2),
        name="swiglu_ffn",
    )(x, g.reshape(1, d), w_gate_up, w_gate_up, w_down, g_final.reshape(1, d))


def kernel(x, ln_mix, w_in, sb_out_norm, swa_out_norm, swa_sinks, w_out, ln_ffn, w_gate_up, w_down, ln_final):
    batch, seq, d = x.shape
    depth = w_in.shape[0]
    xt = x.reshape(batch * seq, d)
    for l in range(depth):
        proj = _norm_proj(xt, ln_mix[l], w_in, l, batch, seq)
        sb_o = _sb_attention(proj, batch, seq).reshape(batch * seq, SB_W)
        sw_o = _swa_attention(proj, swa_sinks[l], batch, seq).reshape(batch * seq, SWA_Q_W)
        xt = _out_proj(sb_o, sw_o, sb_out_norm[l], swa_out_norm[l], w_out, l, xt)
        xt = _ffn(xt, ln_ffn[l], w_gate_up, w_down, l, ln_final, final_norm=(l == depth - 1))
    return xt.reshape(batch, seq, d)
```

```python
import functools

import jax
import jax.numpy as jnp
from jax import lax
from jax.experimental import pallas as pl
from jax.experimental.pallas import tpu as pltpu

D_MODEL = 2048
HEAD_DIM = 64
SB_HEADS = 16
SWA_HEADS = 16
SWA_KV_HEADS = 4
SWA_GROUP = SWA_HEADS // SWA_KV_HEADS
WINDOW = 128
D_FF = 5632
RMS_EPS = 1e-5

SB_W = SB_HEADS * HEAD_DIM
SWA_Q_W = SWA_HEADS * HEAD_DIM
SWA_KV_W = SWA_KV_HEADS * HEAD_DIM
IN_COLS = 3 * SB_W + SWA_Q_W + 2 * SWA_KV_W
Q_SCALE = HEAD_DIM ** -0.5
LOG2E = 1.4426950408889634

LANES = 128
HEAD_PAIR = LANES // HEAD_DIM
VMEM_LIMIT = 56 * 1024 * 1024

F32 = jnp.float32
BF16 = jnp.bfloat16

PROJ_TM, PROJ_TN = 1024, 768
SB_T = 256
SB_SUB = 128
SB_QB = 4
N_SLOT = 3
SWA_T = WINDOW
SWA_QB = 2
OUT_TM = 512
FFN_TM, FFN_TF = 1024, 256


def _params(n_axes):
    return pltpu.CompilerParams(
        dimension_semantics=("arbitrary",) * n_axes,
        vmem_limit_bytes=VMEM_LIMIT)


def _rms(x, g):
    ms = jnp.mean(x * x, axis=-1, keepdims=True)
    return x * lax.rsqrt(ms + RMS_EPS) * g


def _dot(a, b):
    return jnp.dot(a, b, preferred_element_type=F32)


def _dot_nt(a, b):
    return lax.dot_general(a, b, (((1,), (1,)), ((), ())),
                           preferred_element_type=F32)


def _norm_proj_kernel(x_ref, g_ref, w_ref, o_ref, h_ref):
    @pl.when(pl.program_id(1) == 0)
    def _():
        h_ref[...] = _rms(x_ref[...], g_ref[...]).astype(BF16)

    res = _dot(h_ref[...], w_ref[...].astype(BF16))
    for c in range(PROJ_TN // LANES):
        o_ref[0, c] = res[:, c * LANES:(c + 1) * LANES].astype(o_ref.dtype)


def _norm_proj(x, g, w, layer, batch, seq):
    t, d = x.shape
    n = w.shape[2]
    tiles_per_seq = seq // PROJ_TM
    blk_per_tile = PROJ_TN // LANES
    return pl.pallas_call(
        _norm_proj_kernel,
        grid=(t // PROJ_TM, n // PROJ_TN),
        in_specs=[
            pl.BlockSpec((PROJ_TM, d), lambda i, j: (i, 0)),
            pl.BlockSpec((1, d), lambda i, j: (0, 0)),
            pl.BlockSpec((None, d, PROJ_TN), lambda i, j: (layer, 0, j)),
        ],
        out_specs=pl.BlockSpec((1, blk_per_tile, PROJ_TM, LANES),
                               lambda i, j: (i // tiles_per_seq, j, i % tiles_per_seq, 0)),
        out_shape=jax.ShapeDtypeStruct((batch, n // LANES, seq, LANES), BF16),
        scratch_shapes=[pltpu.VMEM((PROJ_TM, d), BF16)],
        compiler_params=_params(2),
        name="norm_in_proj",
    )(x, g.reshape(1, d), w)


def _sb_kernel(q_ref, k_ref, v_ref, o_ref, vt_ref, z_ref, w_ref, acc_ref):
    grid_step = pl.program_id(2)
    n_kblk = vt_ref.shape[0]

    @pl.when(grid_step == 0)
    def _():
        def tr(c, _):
            r0 = pl.multiple_of(c * SB_T, SB_T)
            vt_ref[c] = v_ref[0, 0, pl.ds(r0, SB_T), :].astype(F32).T.astype(BF16)
            return 0
        lax.fori_loop(0, n_kblk, tr, 0)

    r = lax.broadcasted_iota(jnp.int32, (SB_SUB, SB_SUB), 0)
    j2 = lax.broadcasted_iota(jnp.int32, (SB_SUB, SB_SUB), 1)
    tri = jnp.where(j2 >= r, 1.0, 0.0).astype(BF16)

    srow = lax.broadcasted_iota(jnp.int32, (SB_T, SB_T), 0)
    tcol = lax.broadcasted_iota(jnp.int32, (SB_T, SB_T), 1)
    causal = srow < tcol

    n_sub = SB_T // SB_SUB
    heads = range(HEAD_PAIR)

    def q_block(i, rows):
        q_t = (q_ref[0, 0, rows, :].astype(F32) * Q_SCALE).T
        feat = lax.broadcasted_iota(jnp.int32, q_t.shape, 0)
        qt = [jnp.where((feat // HEAD_DIM) == h, q_t, 0.0).astype(BF16) for h in heads]

        def scores(k, slot, h):
            j = jnp.maximum(i - k, 0)
            k0 = pl.multiple_of(j * SB_T, SB_T)
            z_ref[slot, h] = _dot(k_ref[0, 0, pl.ds(k0, SB_T), :], qt[h])

        def values(k, h):
            v_t = vt_ref[i - k, h * HEAD_DIM:(h + 1) * HEAD_DIM, :]
            acc_ref[h] += _dot(v_t, w_ref[h])

        def weights(slot, h, c, diag):
            zt = z_ref[slot, h]
            sp = jnp.maximum(zt, 0.0) + jnp.log(1.0 + jnp.exp2(jnp.abs(zt) * -LOG2E))
            if diag:
                sp = jnp.where(causal, sp, 0.0)
            spb = sp.astype(BF16)
            xs = [None] * n_sub
            for sb in reversed(range(n_sub)):
                r0, r1 = sb * SB_SUB, (sb + 1) * SB_SUB
                cs = _dot(tri, spb[r0:r1])
                xs[sb] = (zt[r0:r1] - cs) - c
                c = c + cs[0:1, :]
            w = jnp.exp(jnp.concatenate(xs, axis=0))
            if diag:
                w = jnp.where(causal, w, 0.0)
            w_ref[h] = w.astype(BF16)
            return c

        acc_ref[...] = jnp.zeros_like(acc_ref)
        for k in range(N_SLOT):
            for h in heads:
                scores(k, k, h)
        cs = tuple(weights(0, h, jnp.zeros((1, SB_T), F32), True) for h in heads)

        def step(k, slot, cs, prefetch):
            if prefetch:
                for h in heads:
                    scores(k + N_SLOT - 1, (slot + N_SLOT - 1) % N_SLOT, h)
            for h in heads:
                values(k - 1, h)
            return tuple(weights(slot, h, cs[h], False) for h in heads)

        def unrolled(m, cs):
            k = N_SLOT * m + 1
            for u in range(N_SLOT):
                cs = step(k + u, (1 + u) % N_SLOT, cs, True)
            return cs

        n_full = i // N_SLOT
        rem = i - n_full * N_SLOT
        cs = lax.fori_loop(0, n_full, unrolled, cs)
        k_rem = N_SLOT * n_full + 1

        for n_rem in range(1, N_SLOT):
            @pl.when(rem == n_rem)
            def _():
                c = cs
                for u in range(n_rem):
                    c = step(k_rem + u, (1 + u) % N_SLOT, c, False)

        for h in heads:
            values(i, h)

        o_ref[0, rows, :] = jnp.concatenate([acc_ref[h] for h in heads], axis=0).T

    for sub in range(SB_QB):
        q_block(grid_step * SB_QB + sub, slice(sub * SB_T, (sub + 1) * SB_T))


def _sb_attention(proj, batch, seq):
    n_pairs = SB_HEADS // HEAD_PAIR
    k_off = SB_W // LANES
    v_off = 2 * SB_W // LANES
    return pl.pallas_call(
        _sb_kernel,
        grid=(batch, n_pairs, seq // (SB_T * SB_QB)),
        in_specs=[
            pl.BlockSpec((1, 1, SB_T * SB_QB, LANES), lambda b, p, i: (b, p, i, 0)),
            pl.BlockSpec((1, 1, seq, LANES), lambda b, p, i: (b, k_off + p, 0, 0)),
            pl.BlockSpec((1, 1, seq, LANES), lambda b, p, i: (b, v_off + p, 0, 0)),
        ],
        out_specs=pl.BlockSpec((1, SB_T * SB_QB, LANES), lambda b, p, i: (b, i, p)),
        out_shape=jax.ShapeDtypeStruct((batch, seq, SB_W), F32),
        scratch_shapes=[
            pltpu.VMEM((seq // SB_T, LANES, SB_T), BF16),
            pltpu.VMEM((N_SLOT, HEAD_PAIR, SB_T, SB_T), F32),
            pltpu.VMEM((HEAD_PAIR, SB_T, SB_T), BF16),
            pltpu.VMEM((HEAD_PAIR, HEAD_DIM, SB_T), F32),
        ],
        compiler_params=_params(3),
        name="stick_breaking_attention",
    )(proj, proj, proj)


def _swa_kernel(sink_ref, q_ref, k_ref, v_ref, o_ref, vt_ref):
    i = pl.program_id(1)
    n_blk = vt_ref.shape[0]

    @pl.when(i == 0)
    def _():
        def tr(c, _):
            r0 = pl.multiple_of(c * SWA_T, SWA_T)
            for m in range(SWA_KV_W // LANES):
                vt_ref[c, m * LANES:(m + 1) * LANES, :] = (
                    v_ref[0, m, pl.ds(r0, SWA_T), :].astype(F32).T.astype(BF16))
            return 0
        lax.fori_loop(0, n_blk, tr, 0)

    def q_block(blk, rows):
        prev = jnp.maximum(blk - 1, 0)
        p0 = pl.multiple_of(prev * SWA_T, SWA_T)
        c0 = pl.multiple_of(blk * SWA_T, SWA_T)
        kband = [jnp.concatenate([k_ref[0, m, pl.ds(p0, SWA_T), :],
                                  k_ref[0, m, pl.ds(c0, SWA_T), :]], axis=0)
                 for m in range(SWA_KV_W // LANES)]
        vtband = jnp.concatenate([vt_ref[prev], vt_ref[blk]], axis=1)

        band = 2 * SWA_T
        c = lax.broadcasted_iota(jnp.int32, (band, SWA_T), 0)
        r = lax.broadcasted_iota(jnp.int32, (band, SWA_T), 1)
        dist = r + SWA_T - c
        valid = (dist >= 0) & (dist < WINDOW) & ((c >= SWA_T) | (blk > 0))
        dist_f = dist.astype(F32)
        neg_inf = jnp.float32(-jnp.inf)

        o_heads = []
        for g in range(SWA_KV_HEADS):
            gs = slice(g * HEAD_DIM, (g + 1) * HEAD_DIM)
            group = range(g * SWA_GROUP, (g + 1) * SWA_GROUP)
            half = lambda n: slice((n % HEAD_PAIR) * HEAD_DIM, (n % HEAD_PAIR + 1) * HEAD_DIM)
            qg = jnp.concatenate(
                [q_ref[0, h // HEAD_PAIR, rows, half(h)] for h in group], axis=0) * Q_SCALE
            zt = _dot_nt(kband[g // HEAD_PAIR][:, half(g)], qg)
            ps, dens = [], []
            for n, h in enumerate(group):
                slope = 2.0 ** (-8.0 * (h + 1.0) / SWA_HEADS)
                sink = sink_ref[h]
                s = jnp.where(valid, zt[:, n * SWA_T:(n + 1) * SWA_T] - slope * dist_f, neg_inf)
                mx = jnp.maximum(jnp.max(s, axis=0, keepdims=True), sink)
                p = jnp.exp(s - mx)
                dens.append(jnp.sum(p, axis=0, keepdims=True) + jnp.exp(sink - mx))
                ps.append(p.astype(BF16))
            o_t = _dot(vtband[gs, :], jnp.concatenate(ps, axis=1))
            for n in range(SWA_GROUP):
                o_heads.append(o_t[:, n * SWA_T:(n + 1) * SWA_T] / dens[n])

        for m in range(SWA_HEADS // HEAD_PAIR):
            pair = jnp.concatenate(o_heads[HEAD_PAIR * m:HEAD_PAIR * (m + 1)], axis=0)
            o_ref[0, rows, m * LANES:(m + 1) * LANES] = pair.T

    for sub in range(SWA_QB):
        q_block(i * SWA_QB + sub, slice(sub * SWA_T, (sub + 1) * SWA_T))


def _swa_attention(proj, sinks, batch, seq):
    q_n, kv_n = SWA_Q_W // LANES, SWA_KV_W // LANES
    q_blk = 3 * SB_W // SWA_Q_W
    k_blk = (3 * SB_W + SWA_Q_W) // SWA_KV_W
    v_blk = k_blk + 1
    return pl.pallas_call(
        _swa_kernel,
        grid=(batch, seq // (SWA_T * SWA_QB)),
        in_specs=[
            pl.BlockSpec(memory_space=pltpu.SMEM),
            pl.BlockSpec((1, q_n, SWA_T * SWA_QB, LANES), lambda b, i: (b, q_blk, i, 0)),
            pl.BlockSpec((1, kv_n, seq, LANES), lambda b, i: (b, k_blk, 0, 0)),
            pl.BlockSpec((1, kv_n, seq, LANES), lambda b, i: (b, v_blk, 0, 0)),
        ],
        out_specs=pl.BlockSpec((1, SWA_T * SWA_QB, SWA_Q_W), lambda b, i: (b, i, 0)),
        out_shape=jax.ShapeDtypeStruct((batch, seq, SWA_Q_W), F32),
        scratch_shapes=[pltpu.VMEM((seq // SWA_T, SWA_KV_W, SWA_T), BF16)],
        compiler_params=_params(2),
        name="sliding_window_attention",
    )(sinks, proj, proj, proj)


def _out_proj_kernel(sb_ref, sw_ref, ga_ref, gb_ref, w_ref, x_ref, o_ref):
    mixed = jnp.concatenate([_rms(sb_ref[...], ga_ref[...]).astype(BF16),
                             _rms(sw_ref[...], gb_ref[...]).astype(BF16)], axis=1)
    o_ref[...] = x_ref[...] + _dot(mixed, w_ref[...].astype(BF16))


def _out_proj(sb_o, sw_o, ga, gb, w, layer, x):
    t, d = x.shape
    mix = SB_W + SWA_Q_W
    return pl.pallas_call(
        _out_proj_kernel,
        grid=(t // OUT_TM,),
        in_specs=[
            pl.BlockSpec((OUT_TM, SB_W), lambda i: (i, 0)),
            pl.BlockSpec((OUT_TM, SWA_Q_W), lambda i: (i, 0)),
            pl.BlockSpec((1, SB_W), lambda i: (0, 0)),
            pl.BlockSpec((1, SWA_Q_W), lambda i: (0, 0)),
            pl.BlockSpec((None, mix, d), lambda i: (layer, 0, 0),
                         pipeline_mode=pl.Buffered(1)),
            pl.BlockSpec((OUT_TM, d), lambda i: (i, 0)),
        ],
        out_specs=pl.BlockSpec((OUT_TM, d), lambda i: (i, 0)),
        out_shape=jax.ShapeDtypeStruct((t, d), F32),
        compiler_params=_params(1),
        name="out_proj",
    )(sb_o, sw_o, ga.reshape(1, SB_W), gb.reshape(1, SWA_Q_W), w, x)


def _ffn_kernel(x_ref, g_ref, wg_ref, wu_ref, wd_ref, gf_ref, o_ref, h_ref, *, final_norm):
    f = pl.program_id(1)

    @pl.when(f == 0)
    def _():
        x = x_ref[...]
        h_ref[...] = _rms(x, g_ref[...]).astype(BF16)
        o_ref[...] = x

    h = h_ref[...]
    gate = _dot(h, wg_ref[...].astype(BF16))
    up = _dot(h, wu_ref[...].astype(BF16))
    act = (gate * jax.nn.sigmoid(gate) * up).astype(BF16)
    o_ref[...] += _dot(act, wd_ref[...].astype(BF16))

    if final_norm:
        @pl.when(f == pl.num_programs(1) - 1)
        def _():
            o_ref[...] = _rms(o_ref[...], gf_ref[...])


def _ffn(x, g, w_gate_up, w_down, layer, g_final, final_norm):
    t, d = x.shape
    n_f = D_FF // FFN_TF
    return pl.pallas_call(
        functools.partial(_ffn_kernel, final_norm=final_norm),
        grid=(t // FFN_TM, n_f),
        in_specs=[
            pl.BlockSpec((FFN_TM, d), lambda i, f: (i, 0)),
            pl.BlockSpec((1, d), lambda i, f: (0, 0)),
            pl.BlockSpec((None, d, FFN_TF), lambda i, f: (layer, 0, f)),
            pl.BlockSpec((None, d, FFN_TF), lambda i, f: (layer, 0, f + n_f)),
            pl.BlockSpec((None, FFN_TF, d), lambda i, f: (layer, f, 0)),
            pl.BlockSpec((1, d), lambda i, f: (0, 0)),
        ],
        out_specs=pl.BlockSpec((FFN_TM, d), lambda i, f: (i, 0)),
        out_shape=jax.ShapeDtypeStruct((t, d), F32),
        scratch_shapes=[pltpu.VMEM((FFN_TM, d), BF16)],
        compiler_params=_params(2),
        name="swiglu_ffn",
    )(x, g.reshape(1, d), w_gate_up, w_gate_up, w_down, g_final.reshape(1, d))


def kernel(x, ln_mix, w_in, sb_out_norm, swa_out_norm, swa_sinks, w_out, ln_ffn, w_gate_up, w_down, ln_final):
    batch, seq, d = x.shape
    depth = w_in.shape[0]
    xt = x.reshape(batch * seq, d)
    for l in range(depth):
        proj = _norm_proj(xt, ln_mix[l], w_in, l, batch, seq)
        sb_o = _sb_attention(proj, batch, seq).reshape(batch * seq, SB_W)
        sw_o = _swa_attention(proj, swa_sinks[l], batch, seq).reshape(batch * seq, SWA_Q_W)
        xt = _out_proj(sb_o, sw_o, sb_out_norm[l], swa_out_norm[l], w_out, l, xt)
        xt = _ffn(xt, ln_ffn[l], w_gate_up, w_down, l, ln_final, final_norm=(l == depth - 1))
    return xt.reshape(batch, seq, d)
```
